```python
import math
import jax, jax.numpy as jnp
from jax import lax
import numpy as np

D_MODEL = 2048
BATCH = 2
SEQ = 16384
DEPTH = 2

RET_HEADS = 8
RET_DK = 128
RET_DV = 256
RET_CHUNK = 128
GDN_HEADS = 8
GDN_DK = 128
GDN_DV = 256
GDN_CHUNK = 64
CONV_WIDTH = 4
SWA_Q_HEADS = 32
SWA_KV_HEADS = 4
SWA_HEAD_DIM = 64
WINDOW = 128
ROPE_THETA = 10000.0
N_GROUPS = 4
EXPERTS_PER_GROUP = 8
N_EXPERTS = N_GROUPS * EXPERTS_PER_GROUP
TOP_K = 2
D_FF_EXPERT = 768
MOE_BLOCK = 128
EPS = 1e-6

GDN_CONV_CH = 2 * GDN_HEADS * GDN_DK + GDN_HEADS * GDN_DV
EVEN_SPLITS = (RET_HEADS * RET_DK, RET_HEADS * RET_DK, RET_HEADS * RET_DV, RET_HEADS * RET_DV,
               GDN_CONV_CH, GDN_HEADS * GDN_DV, GDN_HEADS, GDN_HEADS)
EVEN_IN_DIM = sum(EVEN_SPLITS)
EVEN_MIX_DIM = RET_HEADS * RET_DV + GDN_HEADS * GDN_DV
ODD_SPLITS = (SWA_Q_HEADS * SWA_HEAD_DIM, SWA_KV_HEADS * SWA_HEAD_DIM, SWA_KV_HEADS * SWA_HEAD_DIM)
ODD_IN_DIM = sum(ODD_SPLITS)
ODD_MIX_DIM = SWA_Q_HEADS * SWA_HEAD_DIM

kernel_name = "hybrid_retention_gdn_swa_hmoe"


def rmsnorm(x, g):
    xf = x.astype(jnp.float32)
    y = xf * lax.rsqrt(jnp.mean(xf * xf, axis=-1, keepdims=True) + EPS)
    return (y * g.astype(jnp.float32)).astype(x.dtype)


def l2norm(x):
    return x * lax.rsqrt(jnp.sum(x * x, axis=-1, keepdims=True) + EPS)


def split_cols(x, sizes):
    return jnp.split(x, np.cumsum(sizes)[:-1].tolist(), axis=-1)


def rope(x, pos):
    d = x.shape[-1]
    half = d // 2
    inv_freq = jnp.power(ROPE_THETA, -jnp.arange(half, dtype=jnp.float32) / half)
    ang = pos.astype(jnp.float32)[:, None] * inv_freq[None, :]
    cos = jnp.cos(ang)[None, :, None, :]
    sin = jnp.sin(ang)[None, :, None, :]
    xf = x.astype(jnp.float32)
    x1, x2 = xf[..., :half], xf[..., half:]
    return jnp.concatenate([x1 * cos - x2 * sin, x2 * cos + x1 * sin], axis=-1)


def to_chunks(x, c):
    b, s, h, d = x.shape
    return x.reshape(b, s // c, c, h, d).transpose(1, 0, 3, 2, 4)


def from_chunks(x):
    n, b, h, c, d = x.shape
    return x.transpose(1, 0, 3, 2, 4).reshape(b, n * c, h, d)


def causal_depthwise_conv(x, w):
    k_width = w.shape[0]
    s = x.shape[1]
    xp = jnp.pad(x, ((0, 0), (k_width - 1, 0), (0, 0)))
    return sum(xp[:, j:j + s] * w[j] for j in range(k_width))


def retention_chunkwise(q, k, v):
    b, _, h, dk = q.shape
    dv = v.shape[-1]
    c = RET_CHUNK
    log_gamma = jnp.log1p(-jnp.exp2(-5.0 - jnp.arange(h, dtype=jnp.float32)))
    pos = jnp.arange(c, dtype=jnp.float32)
    rel = pos[:, None] - pos[None, :]
    causal = rel >= 0
    d_in = jnp.where(causal, jnp.exp(log_gamma[:, None, None] * jnp.where(causal, rel, 0.0)), 0.0)
    q_dec = jnp.exp(log_gamma[:, None] * (pos + 1.0))[None, :, :, None]
    k_dec = jnp.exp(log_gamma[:, None] * (c - 1.0 - pos))[None, :, :, None]
    c_dec = jnp.exp(log_gamma * c)[None, :, None, None]
    qc, kc, vc = to_chunks(q, c), to_chunks(k, c), to_chunks(v, c)
    inner = jnp.einsum('nbhij,nbhje->nbhie', jnp.einsum('nbhid,nbhjd->nbhij', qc, kc) * d_in, vc)

    def step(state, inp):
        q_i, k_i, v_i = inp
        cross = jnp.einsum('bhid,bhde->bhie', q_i * q_dec, state)
        state = state * c_dec + jnp.einsum('bhjd,bhje->bhde', k_i * k_dec, v_i)
        return state, cross

    state0 = jnp.zeros((b, h, dk, dv), jnp.float32)
    _, cross = lax.scan(step, state0, (qc, kc, vc))
    return from_chunks(inner + cross)


def gated_delta_rule_chunked(q, k, v, g, beta):
    b, s, h, dk = q.shape
    dv = v.shape[-1]
    c = GDN_CHUNK
    n = s // c
    qc, kc, vc = to_chunks(q, c), to_chunks(k, c), to_chunks(v, c)
    gc = g.reshape(b, n, c, h).transpose(1, 0, 3, 2)
    bc = beta.reshape(b, n, c, h).transpose(1, 0, 3, 2)
    g_cum = jnp.cumsum(gc, axis=-1)
    rel = g_cum[..., :, None] - g_cum[..., None, :]
    causal = jnp.tril(jnp.ones((c, c), bool))
    strict = jnp.tril(jnp.ones((c, c), bool), -1)
    decay = jnp.where(causal, jnp.exp(jnp.where(causal, rel, 0.0)), 0.0)
    k_beta = kc * bc[..., None]
    lower = jnp.where(strict, jnp.einsum('nbhid,nbhjd->nbhij', k_beta, kc) * decay, 0.0)
    a_mat = lower + jnp.eye(c, dtype=jnp.float32)
    rhs = jnp.concatenate([vc * bc[..., None], k_beta * jnp.exp(g_cum)[..., None]], axis=-1)
    sol = lax.linalg.triangular_solve(a_mat.reshape(-1, c, c), rhs.reshape(-1, c, dv + dk),
                                      left_side=True, lower=True, unit_diagonal=True).reshape(rhs.shape)
    u, w = sol[..., :dv], sol[..., dv:]
    attn = jnp.where(causal, jnp.einsum('nbhid,nbhjd->nbhij', qc, kc) * decay, 0.0)
    q_g = qc * jnp.exp(g_cum)[..., None]
    k_tail = kc * jnp.exp(g_cum[..., -1:] - g_cum)[..., None]
    state_dec = jnp.exp(g_cum[..., -1])[..., None, None]

    def step(state, inp):
        u_i, w_i, q_i, k_i, a_i, d_i = inp
        v_new = u_i - jnp.einsum('bhid,bhde->bhie', w_i, state)
        o = jnp.einsum('bhid,bhde->bhie', q_i, state) + jnp.einsum('bhij,bhje->bhie', a_i, v_new)
        state = state * d_i + jnp.einsum('bhjd,bhje->bhde', k_i, v_new)
        return state, o

    state0 = jnp.zeros((b, h, dk, dv), jnp.float32)
    _, o = lax.scan(step, state0, (u, w, q_g, k_tail, attn, state_dec))
    return from_chunks(o)


def sliding_window_attention_sinks(q, k, v, sinks):
    b, s, qh, hd = q.shape
    kvh = k.shape[2]
    grp = qh // kvh
    w = WINDOW
    nb = s // w
    qb = q.reshape(b, nb, w, kvh, grp, hd).transpose(1, 0, 2, 3, 4, 5)

    def with_prev(t):
        tb = t.reshape(b, nb, w, kvh, hd)
        prev = jnp.concatenate([jnp.zeros_like(tb[:, :1]), tb[:, :-1]], axis=1)
        return jnp.concatenate([prev, tb], axis=2).transpose(1, 0, 2, 3, 4)

    kb, vb = with_prev(k), with_prev(v)
    qpos = jnp.arange(w)[:, None] + w
    kpos = jnp.arange(2 * w)[None, :]
    rel = qpos - kpos
    band = (rel >= 0) & (rel < w)
    first = (jnp.arange(nb) == 0)[:, None, None]
    valid = band[None] & ~(first & (kpos[None] < w))
    sink = sinks.astype(jnp.float32).reshape(kvh, grp)[None, :, :, None, None]
    scale = hd ** -0.5

    def block(args):
        q_i, k_i, v_i, m_i = args
        sc = jnp.einsum('bqkgd,bskd->bkgqs', q_i, k_i) * scale
        sc = jnp.where(m_i, sc, -jnp.inf)
        mx = jnp.maximum(jnp.max(sc, axis=-1, keepdims=True), sink)
        p = jnp.exp(sc - mx)
        denom = jnp.sum(p, axis=-1, keepdims=True) + jnp.exp(sink - mx)
        return jnp.einsum('bkgqs,bskd->bqkgd', p / denom, v_i)

    o = lax.map(block, (qb, kb, vb, valid))
    return o.transpose(1, 0, 2, 3, 4, 5).reshape(b, s, qh, hd)


def even_layer_mixer(h, w_in, ret_norm, gdn_conv, gdn_a_log, gdn_dt_bias, gdn_norm, w_out):
    b, s, _ = h.shape
    pos = jnp.arange(s)
    rq, rk, rv, rg, gqkv, gz, gb, ga = split_cols(h @ w_in, EVEN_SPLITS)
    rq = rope(rq.reshape(b, s, RET_HEADS, RET_DK), pos)
    rk = rope(rk.reshape(b, s, RET_HEADS, RET_DK), pos) * RET_DK ** -0.5
    rv = rv.reshape(b, s, RET_HEADS, RET_DV).astype(jnp.float32)
    o_ret = retention_chunkwise(rq, rk, rv)
    o_ret = rmsnorm(o_ret, ret_norm) * jax.nn.silu(rg.reshape(b, s, RET_HEADS, RET_DV).astype(jnp.float32))
    qkv = jax.nn.silu(causal_depthwise_conv(gqkv.astype(jnp.float32), gdn_conv.astype(jnp.float32)))
    gq, gk, gv = split_cols(qkv, (GDN_HEADS * GDN_DK, GDN_HEADS * GDN_DK, GDN_HEADS * GDN_DV))
    gq = l2norm(gq.reshape(b, s, GDN_HEADS, GDN_DK)) * GDN_DK ** -0.5
    gk = l2norm(gk.reshape(b, s, GDN_HEADS, GDN_DK))
    gv = gv.reshape(b, s, GDN_HEADS, GDN_DV)
    beta = jax.nn.sigmoid(gb.astype(jnp.float32))
    g = -jnp.exp(gdn_a_log.astype(jnp.float32)) * jax.nn.softplus(ga.astype(jnp.float32) + gdn_dt_bias.astype(jnp.float32))
    o_gdn = gated_delta_rule_chunked(gq, gk, gv, g, beta)
    o_gdn = rmsnorm(o_gdn, gdn_norm) * jax.nn.silu(gz.reshape(b, s, GDN_HEADS, GDN_DV).astype(jnp.float32))
    mixed = jnp.concatenate([o_ret.reshape(b, s, -1), o_gdn.reshape(b, s, -1)], axis=-1).astype(h.dtype)
    return mixed @ w_out


def odd_layer_mixer(h, w_in, q_norm, k_norm, sinks, w_out):
    b, s, _ = h.shape
    pos = jnp.arange(s)
    q, k, v = split_cols(h @ w_in, ODD_SPLITS)
    q = rope(rmsnorm(q.reshape(b, s, SWA_Q_HEADS, SWA_HEAD_DIM), q_norm), pos)
    k = rope(rmsnorm(k.reshape(b, s, SWA_KV_HEADS, SWA_HEAD_DIM), k_norm), pos)
    v = v.reshape(b, s, SWA_KV_HEADS, SWA_HEAD_DIM).astype(jnp.float32)
    o = sliding_window_attention_sinks(q, k, v, sinks)
    return o.reshape(b, s, -1).astype(h.dtype) @ w_out


def hierarchical_moe(h, w_router_group, w_router_expert, w_gate, w_up, w_down):
    b, s, d = h.shape
    n_tok = b * s
    xt = h.reshape(n_tok, d)
    rows = jnp.arange(n_tok)
    group_logits = (xt @ w_router_group).astype(jnp.float32)
    g_idx = jnp.argmax(group_logits, axis=-1)
    g_p = jax.nn.softmax(group_logits, axis=-1)[rows, g_idx]
    exp_logits = (xt @ w_router_expert).astype(jnp.float32).reshape(n_tok, N_GROUPS, EXPERTS_PER_GROUP)
    in_group = exp_logits[rows, g_idx]
    top_p, top_i = lax.top_k(jax.nn.softmax(in_group, axis=-1), TOP_K)
    gates = g_p[:, None] * top_p / jnp.sum(top_p, axis=-1, keepdims=True)
    n_assign = n_tok * TOP_K
    expert_id = (g_idx[:, None] * EXPERTS_PER_GROUP + top_i).reshape(-1).astype(jnp.int32)
    token_id = jnp.arange(n_assign, dtype=jnp.int32) // TOP_K
    gate_flat = gates.reshape(-1)
    order = jnp.argsort(expert_id)
    e_sorted, tok_sorted, gate_sorted = expert_id[order], token_id[order], gate_flat[order]
    counts = jax.ops.segment_sum(jnp.ones_like(expert_id), expert_id, num_segments=N_EXPERTS)
    start = jnp.cumsum(counts) - counts
    padded = (counts + MOE_BLOCK - 1) // MOE_BLOCK * MOE_BLOCK
    pstart = jnp.cumsum(padded) - padded
    dest = pstart[e_sorted] + (jnp.arange(n_assign, dtype=jnp.int32) - start[e_sorted])
    n_pad = (n_assign + N_EXPERTS * (MOE_BLOCK - 1) + MOE_BLOCK - 1) // MOE_BLOCK * MOE_BLOCK
    n_blk = n_pad // MOE_BLOCK
    tok_pad = jnp.zeros((n_pad,), jnp.int32).at[dest].set(tok_sorted)
    gate_pad = jnp.zeros((n_pad,), jnp.float32).at[dest].set(gate_sorted)
    block_start = jnp.arange(n_blk, dtype=jnp.int32) * MOE_BLOCK
    pend = pstart + padded
    block_expert = jnp.minimum(jnp.sum(pend[None, :] <= block_start[:, None], axis=1), N_EXPERTS - 1)

    def expert_block(args):
        tok_b, e, gate_b = args
        xb = xt[tok_b]
        hid = jax.nn.silu(xb @ w_gate[e]) * (xb @ w_up[e])
        return (hid @ w_down[e]) * gate_b[:, None].astype(xb.dtype)

    yb = lax.map(expert_block, (tok_pad.reshape(n_blk, MOE_BLOCK), block_expert,
                                gate_pad.reshape(n_blk, MOE_BLOCK)))
    y = jnp.zeros_like(xt).at[tok_pad].add(yb.reshape(n_pad, d).astype(xt.dtype))
    return y.reshape(b, s, d)


def setup_inputs(seed: int = 0) -> dict:
    key = jax.random.key(seed)
    ks = jax.random.split(key, 24)
    f32 = jnp.float32
    n_even = (DEPTH + 1) // 2
    n_odd = DEPTH // 2

    def normal(k, shape, scale):
        return jax.random.normal(k, shape, f32) * scale

    def gain(k, shape):
        return 1.0 + 0.02 * jax.random.normal(k, shape, f32)

    dt = jnp.exp(jax.random.uniform(ks[7], (n_even, GDN_HEADS), f32, math.log(1e-3), math.log(1e-1)))
    return {
        "x": normal(ks[0], (BATCH, SEQ, D_MODEL), 1.0),
        "norm_mix": gain(ks[1], (DEPTH, D_MODEL)),
        "norm_ffn": gain(ks[2], (DEPTH, D_MODEL)),
        "even_w_in": normal(ks[3], (n_even, D_MODEL, EVEN_IN_DIM), D_MODEL ** -0.5),
        "ret_norm": gain(ks[4], (n_even, RET_DV)),
        "gdn_conv": normal(ks[5], (n_even, CONV_WIDTH, GDN_CONV_CH), CONV_WIDTH ** -0.5),
        "gdn_a_log": jnp.log(jax.random.uniform(ks[6], (n_even, GDN_HEADS), f32, 1.0, 16.0)),
        "gdn_dt_bias": dt + jnp.log(-jnp.expm1(-dt)),
        "gdn_norm": gain(ks[8], (n_even, GDN_DV)),
        "even_w_out": normal(ks[9], (n_even, EVEN_MIX_DIM, D_MODEL), EVEN_MIX_DIM ** -0.5),
        "odd_w_in": normal(ks[10], (n_odd, D_MODEL, ODD_IN_DIM), D_MODEL ** -0.5),
        "q_norm": gain(ks[11], (n_odd, SWA_HEAD_DIM)),
        "k_norm": gain(ks[12], (n_odd, SWA_HEAD_DIM)),
        "attn_sinks": normal(ks[13], (n_odd, SWA_Q_HEADS), 1.0),
        "odd_w_out": normal(ks[14], (n_odd, ODD_MIX_DIM, D_MODEL), ODD_MIX_DIM ** -0.5),
        "router_group": normal(ks[15], (DEPTH, D_MODEL, N_GROUPS), D_MODEL ** -0.5),
        "router_expert": normal(ks[16], (DEPTH, D_MODEL, N_EXPERTS), D_MODEL ** -0.5),
        "expert_w_gate": normal(ks[17], (DEPTH, N_EXPERTS, D_MODEL, D_FF_EXPERT), D_MODEL ** -0.5),
        "expert_w_up": normal(ks[18], (DEPTH, N_EXPERTS, D_MODEL, D_FF_EXPERT), D_MODEL ** -0.5),
        "expert_w_down": normal(ks[19], (DEPTH, N_EXPERTS, D_FF_EXPERT, D_MODEL), D_FF_EXPERT ** -0.5),
    }


def reference(x, norm_mix, norm_ffn, even_w_in, ret_norm, gdn_conv, gdn_a_log, gdn_dt_bias,
              gdn_norm, even_w_out, odd_w_in, q_norm, k_norm, attn_sinks, odd_w_out,
              router_group, router_expert, expert_w_gate, expert_w_up, expert_w_down):
    for i in range(DEPTH):
        j = i // 2
        h = rmsnorm(x, norm_mix[i])
        if i % 2 == 0:
            x = x + even_layer_mixer(h, even_w_in[j], ret_norm[j], gdn_conv[j], gdn_a_log[j],
                                     gdn_dt_bias[j], gdn_norm[j], even_w_out[j])
        else:
            x = x + odd_layer_mixer(h, odd_w_in[j], q_norm[j], k_norm[j], attn_sinks[j], odd_w_out[j])
        h = rmsnorm(x, norm_ffn[i])
        x = x + hierarchical_moe(h, router_group[i], router_expert[i], expert_w_gate[i],
                                 expert_w_up[i], expert_w_down[i])
    return x
```

```python
import functools

import jax
import jax.numpy as jnp
from jax import lax
from jax.experimental import pallas as pl
from jax.experimental.pallas import tpu as pltpu

F32 = jnp.float32
BF16 = jnp.bfloat16

D_MODEL = 2048
RET_HEADS = 8
RET_DK = 128
RET_DV = 256
GDN_HEADS = 8
GDN_DK = 128
GDN_DV = 256
CONV_WIDTH = 4
SWA_Q_HEADS = 32
SWA_KV_HEADS = 4
SWA_HEAD_DIM = 64
WINDOW = 128
ROPE_THETA = 10000.0
N_GROUPS = 4
EXPERTS_PER_GROUP = 8
N_EXPERTS = N_GROUPS * EXPERTS_PER_GROUP
D_FF_EXPERT = 768
EPS = 1e-6

LANES = 128
SUBLANES = 8
VMEM_LIMIT = 56 * 1024 * 1024

EVEN_MAIN = 2 * RET_HEADS * RET_DK + 2 * RET_HEADS * RET_DV + 2 * GDN_HEADS * GDN_DK + 2 * GDN_HEADS * GDN_DV
ODD_IN = (SWA_Q_HEADS + 2 * SWA_KV_HEADS) * SWA_HEAD_DIM

RET_TILE = 256
GDN_TILE = 128
GDN_CHUNK = 64
MOE_BM = 512
HI = lax.Precision.HIGHEST


def _cparams(sem):
    return pltpu.CompilerParams(dimension_semantics=sem, vmem_limit_bytes=VMEM_LIMIT)


def _sigmoid(x):
    return 1.0 / (1.0 + jnp.exp(-x))


def _nt_dot(a, b, precision=None):
    return lax.dot_general(a, b, (((1,), (1,)), ((), ())), preferred_element_type=F32, precision=precision)


def _tn_dot(a, b, precision=None):
    return lax.dot_general(a, b, (((0,), (0,)), ((), ())), preferred_element_type=F32, precision=precision)


def _rope_table_kernel(cos_ref, sin_ref, *, half):
    rows = cos_ref.shape[0]
    r0 = pl.program_id(0) * rows
    pos = (lax.broadcasted_iota(jnp.int32, (rows, LANES), 0) + r0).astype(F32)
    lane = lax.broadcasted_iota(jnp.int32, (rows, LANES), 1)
    fi = (lane % half).astype(F32)
    inv_freq = jnp.exp(-(fi / half) * jnp.log(ROPE_THETA))
    ang = pos * inv_freq
    first = (lane % (2 * half)) < half
    cos_ref[...] = jnp.cos(ang)
    sin_ref[...] = jnp.where(first, -jnp.sin(ang), jnp.sin(ang))


def _rope_tables(seq, half):
    rows = min(seq, 1024)
    return pl.pallas_call(
        functools.partial(_rope_table_kernel, half=half),
        grid=(seq // rows,),
        out_specs=[pl.BlockSpec((rows, LANES), lambda i: (i, 0))] * 2,
        out_shape=[jax.ShapeDtypeStruct((seq, LANES), F32)] * 2,
        compiler_params=_cparams(("arbitrary",)),
        name="rope_tables",
    )()


def _norm_proj_kernel(x_ref, g_ref, w_ref, *rest, with_aux):
    if with_aux:
        waux_ref, o_ref, aux_ref, h_scr = rest
    else:
        o_ref, h_scr = rest
    j = pl.program_id(1)

    @pl.when(j == 0)
    def _():
        x = x_ref[...]
        ms = jnp.mean(x * x, axis=-1, keepdims=True)
        h = x * lax.rsqrt(ms + EPS) * g_ref[...]
        h_scr[...] = h.astype(BF16)
        if with_aux:
            aux_ref[...] = jnp.dot(h_scr[...], waux_ref[...], preferred_element_type=F32)

    o_ref[...] = jnp.dot(h_scr[...], w_ref[...], preferred_element_type=F32).astype(o_ref.dtype)


def _norm_proj(x, gain, w, w_aux=None, *, tm=1024, tn=1024):
    n, d = x.shape
    n_out = w.shape[1]
    tm = min(tm, n)
    tn = min(tn, n_out)
    with_aux = w_aux is not None
    in_specs = [
        pl.BlockSpec((tm, d), lambda i, j: (i, 0)),
        pl.BlockSpec((1, d), lambda i, j: (0, 0)),
        pl.BlockSpec((d, tn), lambda i, j: (0, j)),
    ]
    out_specs = [pl.BlockSpec((tm, tn), lambda i, j: (i, j))]
    out_shape = [jax.ShapeDtypeStruct((n, n_out), BF16)]
    args = [x, gain.reshape(1, d), w]
    if with_aux:
        in_specs.append(pl.BlockSpec((d, LANES), lambda i, j: (0, 0)))
        out_specs.append(pl.BlockSpec((tm, LANES), lambda i, j: (i, 0)))
        out_shape.append(jax.ShapeDtypeStruct((n, LANES), F32))
        args.append(w_aux)
    res = pl.pallas_call(
        functools.partial(_norm_proj_kernel, with_aux=with_aux),
        grid=(n // tm, n_out // tn),
        in_specs=in_specs,
        out_specs=out_specs,
        out_shape=out_shape,
        scratch_shapes=[pltpu.VMEM((tm, d), BF16)],
        compiler_params=_cparams(("parallel", "arbitrary")),
        name="norm_proj",
    )(*args)
    return res if with_aux else res[0]


def _rope128(x, cos, sin):
    return x * cos + pltpu.roll(x, RET_DK // 2, 1) * sin


def _retention_kernel(q_ref, k_ref, v_ref, g_ref, cos_ref, sin_ref, lg_ref, gn_ref, o_ref,
                      state, dmat, qdec, kdec):
    t = pl.program_id(2)
    tile = q_ref.shape[0]
    lg = lg_ref[0][:, :1]

    @pl.when(t == 0)
    def _():
        state[...] = jnp.zeros_like(state)
        ri = lax.broadcasted_iota(jnp.int32, (tile, tile), 0)
        ci = lax.broadcasted_iota(jnp.int32, (tile, tile), 1)
        causal = ri >= ci
        rel = jnp.where(causal, ri - ci, 0).astype(F32)
        dmat[...] = jnp.where(causal, jnp.exp(lg * rel), 0.0)
        pos = lax.broadcasted_iota(jnp.int32, (tile, RET_DK), 0).astype(F32)
        qdec[...] = jnp.exp(lg * (pos + 1.0))
        kdec[...] = jnp.exp(lg * (tile - 1.0 - pos))

    cos = cos_ref[...]
    sin = sin_ref[...]
    q = _rope128(q_ref[...].astype(F32), cos, sin)
    k = _rope128(k_ref[...].astype(F32), cos, sin) * (RET_DK ** -0.5)
    v = v_ref[...]
    s = _nt_dot(q.astype(BF16), k.astype(BF16))
    p = (s * dmat[...]).astype(BF16)
    o = jnp.dot(p, v, preferred_element_type=F32)
    st = state[...]
    o = o + jnp.dot((q * qdec[...]).astype(BF16), st.astype(BF16), preferred_element_type=F32)
    state[...] = st * jnp.exp(lg * float(tile)) + _tn_dot((k * kdec[...]).astype(BF16), v)
    ms = jnp.mean(o * o, axis=-1, keepdims=True)
    y = o * lax.rsqrt(ms + EPS) * gn_ref[...]
    gate = g_ref[...].astype(F32)
    o_ref[...] = (y * (gate * _sigmoid(gate))).astype(o_ref.dtype)


def _retention(proj, cos, sin, log_gamma, ret_norm, batch, seq):
    tile = min(RET_TILE, seq)
    nt = seq // tile
    qb = 0
    kb = RET_HEADS
    vb = (2 * RET_HEADS * RET_DK) // RET_DV
    gb = vb + RET_HEADS
    row = lambda b, h, t: b * nt + t
    return pl.pallas_call(
        _retention_kernel,
        grid=(batch, RET_HEADS, nt),
        in_specs=[
            pl.BlockSpec((tile, RET_DK), lambda b, h, t: (row(b, h, t), qb + h)),
            pl.BlockSpec((tile, RET_DK), lambda b, h, t: (row(b, h, t), kb + h)),
            pl.BlockSpec((tile, RET_DV), lambda b, h, t: (row(b, h, t), vb + h)),
            pl.BlockSpec((tile, RET_DV), lambda b, h, t: (row(b, h, t), gb + h)),
            pl.BlockSpec((tile, LANES), lambda b, h, t: (t, 0)),
            pl.BlockSpec((tile, LANES), lambda b, h, t: (t, 0)),
            pl.BlockSpec((1, 1, LANES), lambda b, h, t: (h, 0, 0)),
            pl.BlockSpec((1, RET_DV), lambda b, h, t: (0, 0)),
        ],
        out_specs=pl.BlockSpec((tile, RET_DV), lambda b, h, t: (row(b, h, t), h)),
        out_shape=jax.ShapeDtypeStruct((batch * seq, RET_HEADS * RET_DV), BF16),
        scratch_shapes=[
            pltpu.VMEM((RET_DK, RET_DV), F32),
            pltpu.VMEM((tile, tile), F32),
            pltpu.VMEM((tile, RET_DK), F32),
            pltpu.VMEM((tile, RET_DK), F32),
        ],
        compiler_params=_cparams(("parallel", "parallel", "arbitrary")),
        name="retention",
    )(proj, proj, proj, proj, cos, sin, log_gamma, ret_norm.reshape(1, RET_DV))


def _conv_silu(xbuf, x_ref, w_ref):
    tile = x_ref.shape[0]
    xbuf[SUBLANES:SUBLANES + tile, :] = x_ref[...].astype(F32)
    w = w_ref[...]
    base = SUBLANES - (CONV_WIDTH - 1)
    acc = xbuf[base:base + tile, :] * w[0:1, :]
    for j in range(1, CONV_WIDTH):
        acc = acc + xbuf[base + j:base + j + tile, :] * w[j:j + 1, :]
    xbuf[0:SUBLANES, :] = xbuf[tile:tile + SUBLANES, :]
    return acc * _sigmoid(acc)


def _l2norm(x):
    return x * lax.rsqrt(jnp.sum(x * x, axis=-1, keepdims=True) + EPS)


def _softplus(x):
    return jnp.maximum(x, 0.0) + jnp.log1p(jnp.exp(-jnp.abs(x)))


def _unit_lower_inverse(low):
    c = low.shape[0]
    eye = (lax.broadcasted_iota(jnp.int32, (c, c), 0) == lax.broadcasted_iota(jnp.int32, (c, c), 1)).astype(F32)
    inv = eye - low
    power = low
    span = 2
    while span < c:
        power = jnp.dot(power, power, preferred_element_type=F32, precision=HI)
        inv = inv + jnp.dot(inv, power, preferred_element_type=F32, precision=HI)
        span *= 2
    return inv


def _gdn_kernel(xq_ref, xk_ref, xv_ref, z_ref, aux_ref, wq_ref, wk_ref, wv_ref, alog_ref, dtb_ref,
                gn_ref, o_ref, state, qbuf, kbuf, vbuf):
    h = pl.program_id(1)
    t = pl.program_id(2)
    tile = xq_ref.shape[0]
    c = GDN_CHUNK

    @pl.when(t == 0)
    def _():
        state[...] = jnp.zeros_like(state)
        qbuf[0:SUBLANES, :] = jnp.zeros((SUBLANES, GDN_DK), F32)
        kbuf[0:SUBLANES, :] = jnp.zeros((SUBLANES, GDN_DK), F32)
        vbuf[0:SUBLANES, :] = jnp.zeros((SUBLANES, GDN_DV), F32)

    q = _l2norm(_conv_silu(qbuf, xq_ref, wq_ref)) * (GDN_DK ** -0.5)
    k = _l2norm(_conv_silu(kbuf, xk_ref, wk_ref))
    v = _conv_silu(vbuf, xv_ref, wv_ref)

    aux = aux_ref[...]
    lane = lax.broadcasted_iota(jnp.int32, (tile, LANES), 1)
    beta_all = _sigmoid(aux)
    g_all = -jnp.exp(alog_ref[...]) * _softplus(aux + dtb_ref[...])
    beta = jnp.sum(jnp.where(lane == h, beta_all, 0.0), axis=1, keepdims=True)
    g = jnp.sum(jnp.where(lane == h + GDN_HEADS, g_all, 0.0), axis=1, keepdims=True)

    ri = lax.broadcasted_iota(jnp.int32, (c, c), 0)
    ci = lax.broadcasted_iota(jnp.int32, (c, c), 1)
    causal = ri >= ci
    strict = ri > ci
    tril = causal.astype(F32)
    triu = (ri <= ci).astype(F32)

    outs = []
    for i in range(tile // c):
        sl = slice(i * c, (i + 1) * c)
        qc, kc, vc, bc = q[sl], k[sl], v[sl], beta[sl]
        gb = jnp.broadcast_to(g[sl], (c, LANES))
        gcum = jnp.dot(tril, gb, preferred_element_type=F32, precision=HI)
        gcum_t = _tn_dot(gb[:, :c], triu, precision=HI)
        rel = gcum[:, :c] - gcum_t
        decay = jnp.where(causal, jnp.exp(jnp.where(causal, rel, 0.0)), 0.0)
        eg = jnp.exp(gcum)
        kb = kc * bc
        kcb = kc.astype(BF16)
        low = jnp.where(strict, _nt_dot(kb.astype(BF16), kcb) * decay, 0.0)
        inv = _unit_lower_inverse(low).astype(BF16)
        u = jnp.dot(inv, (vc * bc).astype(BF16), preferred_element_type=F32)
        w = jnp.dot(inv, (kb * eg).astype(BF16), preferred_element_type=F32)
        attn = jnp.where(causal, _nt_dot(qc.astype(BF16), kcb) * decay, 0.0)
        g_last = gcum[c - 1:c, :]
        k_tail = kc * jnp.exp(g_last - gcum)
        st = state[...]
        stb = st.astype(BF16)
        v_new = u - jnp.dot(w.astype(BF16), stb, preferred_element_type=F32)
        v_new_b = v_new.astype(BF16)
        o = jnp.dot((qc * eg).astype(BF16), stb, preferred_element_type=F32)
        o = o + jnp.dot(attn.astype(BF16), v_new_b, preferred_element_type=F32)
        state[...] = st * jnp.exp(g_last[:, :1]) + _tn_dot(k_tail.astype(BF16), v_new_b)
        outs.append(o)
    o = jnp.concatenate(outs, axis=0) if len(outs) > 1 else outs[0]
    ms = jnp.mean(o * o, axis=-1, keepdims=True)
    y = o * lax.rsqrt(ms + EPS) * gn_ref[...]
    z = z_ref[...].astype(F32)
    o_ref[...] = (y * (z * _sigmoid(z))).astype(o_ref.dtype)


def _gdn(proj, aux, conv_w, alog_row, dtb_row, gdn_norm, batch, seq):
    tile = min(GDN_TILE, seq)
    nt = seq // tile
    base = 2 * RET_HEADS * RET_DK + 2 * RET_HEADS * RET_DV
    qb = base // GDN_DK
    kb = qb + GDN_HEADS
    vb = (base + 2 * GDN_HEADS * GDN_DK) // GDN_DV
    zb = vb + GDN_HEADS
    row = lambda b, h, t: b * nt + t
    return pl.pallas_call(
        _gdn_kernel,
        grid=(batch, GDN_HEADS, nt),
        in_specs=[
            pl.BlockSpec((tile, GDN_DK), lambda b, h, t: (row(b, h, t), qb + h)),
            pl.BlockSpec((tile, GDN_DK), lambda b, h, t: (row(b, h, t), kb + h)),
            pl.BlockSpec((tile, GDN_DV), lambda b, h, t: (row(b, h, t), vb + h)),
            pl.BlockSpec((tile, GDN_DV), lambda b, h, t: (row(b, h, t), zb + h)),
            pl.BlockSpec((tile, LANES), lambda b, h, t: (row(b, h, t), 0)),
            pl.BlockSpec((CONV_WIDTH, GDN_DK), lambda b, h, t: (0, h)),
            pl.BlockSpec((CONV_WIDTH, GDN_DK), lambda b, h, t: (0, GDN_HEADS + h)),
            pl.BlockSpec((CONV_WIDTH, GDN_DV), lambda b, h, t: (0, (2 * GDN_HEADS * GDN_DK) // GDN_DV + h)),
            pl.BlockSpec((1, LANES), lambda b, h, t: (0, 0)),
            pl.BlockSpec((1, LANES), lambda b, h, t: (0, 0)),
            pl.BlockSpec((1, GDN_DV), lambda b, h, t: (0, 0)),
        ],
        out_specs=pl.BlockSpec((tile, GDN_DV), lambda b, h, t: (row(b, h, t), h)),
        out_shape=jax.ShapeDtypeStruct((batch * seq, GDN_HEADS * GDN_DV), BF16),
        scratch_shapes=[
            pltpu.VMEM((GDN_DK, GDN_DV), F32),
            pltpu.VMEM((tile + SUBLANES, GDN_DK), F32),
            pltpu.VMEM((tile + SUBLANES, GDN_DK), F32),
            pltpu.VMEM((tile + SUBLANES, GDN_DV), F32),
        ],
        compiler_params=_cparams(("parallel", "parallel", "arbitrary")),
        name="gated_deltanet",
    )(proj, proj, proj, proj, aux, conv_w, conv_w, conv_w, alog_row, dtb_row, gdn_norm.reshape(1, GDN_DV))


def _rope64(x, cos, sin):
    half = SWA_HEAD_DIM // 2
    lane = lax.broadcasted_iota(jnp.int32, x.shape, x.ndim - 1)
    first = (lane % SWA_HEAD_DIM) < half
    rot = jnp.where(first, pltpu.roll(x, LANES - half, x.ndim - 1), pltpu.roll(x, half, x.ndim - 1))
    return x * cos + rot * sin


def _head_rmsnorm(x, gain, seg_ones):
    ss = jnp.dot(x * x, seg_ones, preferred_element_type=F32, precision=HI)
    return x * lax.rsqrt(ss * (1.0 / SWA_HEAD_DIM) + EPS) * gain


def _swa_kernel(sink_ref, q_ref, k_ref, v_ref, cos_ref, sin_ref, qn_ref, kn_ref, seg_ref, o_ref,
                kprev, vprev):
    i = pl.program_id(1)
    w = WINDOW
    pairs = SWA_KV_HEADS // 2
    grp = SWA_Q_HEADS // SWA_KV_HEADS
    qchunks = grp * SWA_HEAD_DIM // LANES

    @pl.when(i == 0)
    def _():
        kprev[...] = jnp.zeros_like(kprev)
        vprev[...] = jnp.zeros_like(vprev)

    cos = cos_ref[...]
    sin = sin_ref[...]
    seg = seg_ref[...]
    lane = lax.broadcasted_iota(jnp.int32, (w, LANES), 1)
    lo_mask = lane < SWA_HEAD_DIM

    k_lo, k_hi, v_lo, v_hi = [], [], [], []
    for c in range(pairs):
        kc = k_ref[:, c * LANES:(c + 1) * LANES].astype(F32)
        kc = _rope64(_head_rmsnorm(kc, kn_ref[...], seg), cos, sin)
        vc = v_ref[:, c * LANES:(c + 1) * LANES].astype(F32)
        for src, lo_list, hi_list in ((kc, k_lo, k_hi), (vc, v_lo, v_hi)):
            a_lo = jnp.where(lo_mask, src, 0.0)
            b_hi = jnp.where(lo_mask, 0.0, src)
            lo_list += [a_lo, pltpu.roll(b_hi, SWA_HEAD_DIM, 1)]
            hi_list += [pltpu.roll(a_lo, SWA_HEAD_DIM, 1), b_hi]

    qpos = lax.broadcasted_iota(jnp.int32, (w, 2 * w), 0) + w
    kpos = lax.broadcasted_iota(jnp.int32, (w, 2 * w), 1)
    rel = qpos - kpos
    first_key = jnp.where(i > 0, 0, w)
    valid = (rel >= 0) & (rel < w) & (kpos >= first_key)
    scale = SWA_HEAD_DIM ** -0.5
    cos_q = jnp.concatenate([cos] * qchunks, axis=0)
    sin_q = jnp.concatenate([sin] * qchunks, axis=0)

    for kvh in range(SWA_KV_HEADS):
        q0 = kvh * grp * SWA_HEAD_DIM
        q2 = jnp.concatenate(
            [q_ref[:, q0 + c * LANES:q0 + (c + 1) * LANES].astype(F32) for c in range(qchunks)], axis=0)
        q2 = _head_rmsnorm(q2, qn_ref[...], seg)
        q2 = (_rope64(q2, cos_q, sin_q) * scale).astype(BF16)
        kl = jnp.concatenate([kprev[0, kvh], k_lo[kvh].astype(BF16)], axis=0)
        kh = jnp.concatenate([kprev[1, kvh], k_hi[kvh].astype(BF16)], axis=0)
        vl = jnp.concatenate([vprev[0, kvh], v_lo[kvh].astype(BF16)], axis=0)
        vh = jnp.concatenate([vprev[1, kvh], v_hi[kvh].astype(BF16)], axis=0)
        acc = None
        for par, kk, vv in ((0, kl, vl), (1, kh, vh)):
            sc = _nt_dot(q2, kk)
            probs = []
            for c in range(qchunks):
                sink = sink_ref[kvh * grp + 2 * c + par]
                s = jnp.where(valid, sc[c * w:(c + 1) * w, :], -jnp.inf)
                mx = jnp.maximum(jnp.max(s, axis=-1, keepdims=True), sink)
                p = jnp.exp(s - mx)
                denom = jnp.sum(p, axis=-1, keepdims=True) + jnp.exp(sink - mx)
                probs.append((p / denom).astype(BF16))
            pv = jnp.dot(jnp.concatenate(probs, axis=0), vv, preferred_element_type=F32)
            acc = pv if acc is None else acc + pv
        for c in range(qchunks):
            o_ref[:, q0 + c * LANES:q0 + (c + 1) * LANES] = acc[c * w:(c + 1) * w, :].astype(o_ref.dtype)

    for kvh in range(SWA_KV_HEADS):
        kprev[0, kvh] = k_lo[kvh].astype(BF16)
        kprev[1, kvh] = k_hi[kvh].astype(BF16)
        vprev[0, kvh] = v_lo[kvh].astype(BF16)
        vprev[1, kvh] = v_hi[kvh].astype(BF16)


def _swa(proj, cos, sin, q_norm, k_norm, sinks, batch, seq):
    w = WINDOW
    nb = seq // w
    qw = SWA_Q_HEADS * SWA_HEAD_DIM
    kvw = SWA_KV_HEADS * SWA_HEAD_DIM
    kblk = qw // kvw
    reps = LANES // SWA_HEAD_DIM
    seg = (jnp.arange(LANES)[:, None] // SWA_HEAD_DIM == jnp.arange(LANES)[None, :] // SWA_HEAD_DIM).astype(F32)
    return pl.pallas_call(
        _swa_kernel,
        grid=(batch, nb),
        in_specs=[
            pl.BlockSpec(memory_space=pltpu.SMEM),
            pl.BlockSpec((w, qw), lambda b, i: (b * nb + i, 0)),
            pl.BlockSpec((w, kvw), lambda b, i: (b * nb + i, kblk)),
            pl.BlockSpec((w, kvw), lambda b, i: (b * nb + i, kblk + 1)),
            pl.BlockSpec((w, LANES), lambda b, i: (i, 0)),
            pl.BlockSpec((w, LANES), lambda b, i: (i, 0)),
            pl.BlockSpec((1, LANES), lambda b, i: (0, 0)),
            pl.BlockSpec((1, LANES), lambda b, i: (0, 0)),
            pl.BlockSpec((LANES, LANES), lambda b, i: (0, 0)),
        ],
        out_specs=pl.BlockSpec((w, qw), lambda b, i: (b * nb + i, 0)),
        out_shape=jax.ShapeDtypeStruct((batch * seq, qw), BF16),
        scratch_shapes=[
            pltpu.VMEM((2, SWA_KV_HEADS, w, LANES), BF16),
            pltpu.VMEM((2, SWA_KV_HEADS, w, LANES), BF16),
        ],
        compiler_params=_cparams(("parallel", "arbitrary")),
        name="swa",
    )(sinks.astype(F32), proj, proj, proj, cos, sin,
      jnp.tile(q_norm.astype(F32), reps).reshape(1, LANES),
      jnp.tile(k_norm.astype(F32), reps).reshape(1, LANES), seg)


def _route(logits):
    tm = logits.shape[0]
    lane = lax.broadcasted_iota(jnp.int32, (tm, LANES), 1)
    lane_f = lane.astype(F32)
    big = float(LANES)
    neg = -jnp.inf
    gl = jnp.where(lane < N_GROUPS, logits, neg)
    gmax = jnp.max(gl, axis=1, keepdims=True)
    gidx = jnp.min(jnp.where(gl == gmax, lane_f, big), axis=1, keepdims=True)
    g_p = 1.0 / jnp.sum(jnp.exp(gl - gmax), axis=1, keepdims=True)
    egroup = lax.shift_right_arithmetic(lane - N_GROUPS, 3).astype(F32)
    in_group = (lane >= N_GROUPS) & (lane < N_GROUPS + N_EXPERTS) & (egroup == gidx)
    el = jnp.where(in_group, logits, neg)
    emax = jnp.max(el, axis=1, keepdims=True)
    ee = jnp.exp(el - emax)
    prob = ee / jnp.sum(ee, axis=1, keepdims=True)
    pm = jnp.where(in_group, prob, -1.0)
    p1 = jnp.max(pm, axis=1, keepdims=True)
    i1 = jnp.min(jnp.where(pm == p1, lane_f, big), axis=1, keepdims=True)
    pm2 = jnp.where(lane_f == i1, -1.0, pm)
    p2 = jnp.max(pm2, axis=1, keepdims=True)
    i2 = jnp.min(jnp.where(pm2 == p2, lane_f, big), axis=1, keepdims=True)
    den = p1 + p2
    out = jnp.where(lane == 0, i1 - N_GROUPS, 0.0)
    out = jnp.where(lane == 1, i2 - N_GROUPS, out)
    out = jnp.where(lane == 2, g_p * p1 / den, out)
    out = jnp.where(lane == 3, g_p * p2 / den, out)
    return out


def _out_proj_kernel(*refs, n_act):
    acts = refs[:n_act]
    ws = refs[n_act:2 * n_act]
    x_ref, g_ref, wr_ref, x1_ref, h_ref, route_ref = refs[2 * n_act:]
    y = x_ref[...]
    for a, w in zip(acts, ws):
        y = y + jnp.dot(a[...], w[...], preferred_element_type=F32)
    x1_ref[...] = y
    ms = jnp.mean(y * y, axis=-1, keepdims=True)
    h = y * lax.rsqrt(ms + EPS) * g_ref[...]
    h_ref[...] = h.astype(h_ref.dtype)
    route_ref[...] = _route(jnp.dot(h, wr_ref[...], preferred_element_type=F32, precision=HI))


def _out_proj(acts, ws, x, gain, w_router, *, tm=256):
    n, d = x.shape
    tm = min(tm, n)
    n_act = len(acts)
    const = lambda i: (0, 0)
    in_specs = [pl.BlockSpec((tm, a.shape[1]), lambda i: (i, 0)) for a in acts]
    in_specs += [pl.BlockSpec(w.shape, const, pipeline_mode=pl.Buffered(1)) for w in ws]
    in_specs += [
        pl.BlockSpec((tm, d), lambda i: (i, 0)),
        pl.BlockSpec((1, d), const),
        pl.BlockSpec((d, LANES), const),
    ]
    return pl.pallas_call(
        functools.partial(_out_proj_kernel, n_act=n_act),
        grid=(n // tm,),
        in_specs=in_specs,
        out_specs=[
            pl.BlockSpec((tm, d), lambda i: (i, 0)),
            pl.BlockSpec((tm, d), lambda i: (i, 0)),
            pl.BlockSpec((tm, LANES), lambda i: (i, 0)),
        ],
        out_shape=[
            jax.ShapeDtypeStruct((n, d), F32),
            jax.ShapeDtypeStruct((n, d), BF16),
            jax.ShapeDtypeStruct((n, LANES), F32),
        ],
        compiler_params=_cparams(("parallel",)),
        name="out_proj_router",
    )(*acts, *ws, x, gain.reshape(1, d), w_router)


def _expert_kernel(be_ref, nu_ref, x_ref, wg_ref, wu_ref, wd_ref, o_ref):
    i = pl.program_id(0)

    @pl.when(i < nu_ref[0])
    def _():
        x = x_ref[...]
        g = jnp.dot(x, wg_ref[0], preferred_element_type=F32)
        u = jnp.dot(x, wu_ref[0], preferred_element_type=F32)
        hid = (g * _sigmoid(g) * u).astype(BF16)
        o_ref[...] = jnp.dot(hid, wd_ref[0], preferred_element_type=F32).astype(o_ref.dtype)

    @pl.when(i >= nu_ref[0])
    def _():
        o_ref[...] = jnp.zeros_like(o_ref)


def _expert_mlp(xs, block_expert, n_used, w_gate, w_up, w_down):
    n_pad, d = xs.shape
    bm = MOE_BM
    ff = w_gate.shape[2]
    row = lambda i, be, nu: (jnp.minimum(i, nu[0] - 1), 0)
    grid_spec = pltpu.PrefetchScalarGridSpec(
        num_scalar_prefetch=2,
        grid=(n_pad // bm,),
        in_specs=[
            pl.BlockSpec((bm, d), row),
            pl.BlockSpec((1, d, ff), lambda i, be, nu: (be[i], 0, 0)),
            pl.BlockSpec((1, d, ff), lambda i, be, nu: (be[i], 0, 0)),
            pl.BlockSpec((1, ff, d), lambda i, be, nu: (be[i], 0, 0)),
        ],
        out_specs=pl.BlockSpec((bm, d), lambda i, be, nu: (i, 0)),
    )
    return pl.pallas_call(
        _expert_kernel,
        grid_spec=grid_spec,
        out_shape=jax.ShapeDtypeStruct((n_pad, d), BF16),
        compiler_params=_cparams(("arbitrary",)),
        name="expert_mlp",
    )(block_expert, n_used, xs, w_gate, w_up, w_down)


def _moe(x1, h2, route, w_gate, w_up, w_down):
    n, d = x1.shape
    bm = MOE_BM
    n_assign = 2 * n
    expert_id = route[:, 0:2].astype(jnp.int32).reshape(-1)
    gates = route[:, 2:4]
    onehot = (expert_id[:, None] == jnp.arange(N_EXPERTS, dtype=jnp.int32)[None, :]).astype(jnp.int32)
    csum = jnp.cumsum(onehot, axis=0)
    rank = jnp.sum(csum * onehot, axis=1) - 1
    counts = csum[-1]
    padded = (counts + bm - 1) // bm * bm
    pend = jnp.cumsum(padded)
    pstart = pend - padded
    dest = pstart[expert_id] + rank
    n_pad = (n_assign + N_EXPERTS * (bm - 1) + bm - 1) // bm * bm
    n_blk = n_pad // bm
    token_id = jnp.arange(n_assign, dtype=jnp.int32) // 2
    tok_pad = jnp.zeros((n_pad,), jnp.int32).at[dest].set(token_id)
    block_start = jnp.arange(n_blk, dtype=jnp.int32) * bm
    block_expert = jnp.minimum(jnp.sum(pend[None, :] <= block_start[:, None], axis=1), N_EXPERTS - 1)
    n_used = (pend[-1] // bm).astype(jnp.int32).reshape(1)
    xs = jnp.take(h2, tok_pad, axis=0)
    yb = _expert_mlp(xs, block_expert.astype(jnp.int32), n_used, w_gate, w_up, w_down)
    dest2 = dest.reshape(n, 2)
    y0 = jnp.take(yb, dest2[:, 0], axis=0).astype(F32)
    y1 = jnp.take(yb, dest2[:, 1], axis=0).astype(F32)
    return x1 + gates[:, 0:1] * y0 + gates[:, 1:2] * y1


def _router_weights(w_group, w_expert):
    d = w_group.shape[0]
    pad = jnp.zeros((d, LANES - N_GROUPS - N_EXPERTS), F32)
    return jnp.concatenate([w_group.astype(F32), w_expert.astype(F32), pad], axis=1)


def kernel(x, norm_mix, norm_ffn, even_w_in, ret_norm, gdn_conv, gdn_a_log, gdn_dt_bias, gdn_norm,
           even_w_out, odd_w_in, q_norm, k_norm, attn_sinks, odd_w_out, router_group, router_expert,
           expert_w_gate, expert_w_up, expert_w_down):
    batch, seq, d = x.shape
    n = batch * seq
    xt = x.reshape(n, d)

    w_in = even_w_in[0]
    w_main = w_in[:, :EVEN_MAIN].astype(BF16)
    w_aux = jnp.pad(w_in[:, EVEN_MAIN:], ((0, 0), (0, LANES - 2 * GDN_HEADS))).astype(BF16)
    proj, aux = _norm_proj(xt, norm_mix[0], w_main, w_aux)
    cos_r, sin_r = _rope_tables(seq, RET_DK // 2)
    log_gamma = jnp.log1p(-jnp.exp2(-5.0 - jnp.arange(RET_HEADS, dtype=F32)))
    log_gamma = jnp.broadcast_to(log_gamma[:, None, None], (RET_HEADS, 1, LANES))
    o_ret = _retention(proj, cos_r, sin_r, log_gamma, ret_norm[0].astype(F32), batch, seq)
    lane_pad = (GDN_HEADS, LANES - 2 * GDN_HEADS)
    alog_row = jnp.pad(gdn_a_log[0].astype(F32), lane_pad).reshape(1, LANES)
    dtb_row = jnp.pad(gdn_dt_bias[0].astype(F32), lane_pad).reshape(1, LANES)
    o_gdn = _gdn(proj, aux, gdn_conv[0].astype(F32), alog_row, dtb_row, gdn_norm[0].astype(F32), batch, seq)
    w_out = even_w_out[0].astype(BF16)
    split = RET_HEADS * RET_DV
    x1, h2, route = _out_proj([o_ret, o_gdn], [w_out[:split], w_out[split:]], xt, norm_ffn[0],
                              _router_weights(router_group[0], router_expert[0]))
    xt = _moe(x1, h2, route, expert_w_gate[0].astype(BF16), expert_w_up[0].astype(BF16),
              expert_w_down[0].astype(BF16))

    proj = _norm_proj(xt, norm_mix[1], odd_w_in[0].astype(BF16), tn=ODD_IN // 2)
    cos_s, sin_s = _rope_tables(seq, SWA_HEAD_DIM // 2)
    o_swa = _swa(proj, cos_s, sin_s, q_norm[0], k_norm[0], attn_sinks[0], batch, seq)
    x1, h2, route = _out_proj([o_swa], [odd_w_out[0].astype(BF16)], xt, norm_ffn[1],
                              _router_weights(router_group[1], router_expert[1]))
    xt = _moe(x1, h2, route, expert_w_gate[1].astype(BF16), expert_w_up[1].astype(BF16),
              expert_w_down[1].astype(BF16))
    return xt.reshape(batch, seq, d)
```

```python
import functools

import jax
import jax.numpy as jnp
from jax import lax
from jax.experimental import pallas as pl
from jax.experimental.pallas import tpu as pltpu

F32 = jnp.float32
BF16 = jnp.bfloat16

D_MODEL = 2048
RET_HEADS = 8
RET_DK = 128
RET_DV = 256
GDN_HEADS = 8
GDN_DK = 128
GDN_DV = 256
CONV_WIDTH = 4
SWA_Q_HEADS = 32
SWA_KV_HEADS = 4
SWA_HEAD_DIM = 64
WINDOW = 128
ROPE_THETA = 10000.0
N_GROUPS = 4
EXPERTS_PER_GROUP = 8
N_EXPERTS = N_GROUPS * EXPERTS_PER_GROUP
D_FF_EXPERT = 768
EPS = 1e-6

LANES = 128
SUBLANES = 8
VMEM_LIMIT = 56 * 1024 * 1024

EVEN_MAIN = 2 * RET_HEADS * RET_DK + 2 * RET_HEADS * RET_DV + 2 * GDN_HEADS * GDN_DK + 2 * GDN_HEADS * GDN_DV
ODD_IN = (SWA_Q_HEADS + 2 * SWA_KV_HEADS) * SWA_HEAD_DIM

RET_TILE = 256
GDN_TILE = 256
GDN_CHUNK = 128
GDN_HEADS_PER_STEP = 2
MOE_BM = 512


def _cparams(sem):
    return pltpu.CompilerParams(dimension_semantics=sem, vmem_limit_bytes=VMEM_LIMIT)


def _sigmoid(x):
    return 1.0 / (1.0 + jnp.exp(-x))


def _nt_dot(a, b):
    return lax.dot_general(a, b, (((1,), (1,)), ((), ())), preferred_element_type=F32)


def _tn_dot(a, b):
    return lax.dot_general(a, b, (((0,), (0,)), ((), ())), preferred_element_type=F32)


def _rope_table_kernel(cos_ref, sin_ref, *, half):
    rows = cos_ref.shape[0]
    r0 = pl.program_id(0) * rows
    pos = (lax.broadcasted_iota(jnp.int32, (rows, LANES), 0) + r0).astype(F32)
    lane = lax.broadcasted_iota(jnp.int32, (rows, LANES), 1)
    fi = (lane % half).astype(F32)
    inv_freq = jnp.exp(-(fi / half) * jnp.log(ROPE_THETA))
    ang = pos * inv_freq
    first = (lane % (2 * half)) < half
    cos_ref[...] = jnp.cos(ang)
    sin_ref[...] = jnp.where(first, -jnp.sin(ang), jnp.sin(ang))


def _rope_tables(seq, half):
    rows = min(seq, 1024)
    return pl.pallas_call(
        functools.partial(_rope_table_kernel, half=half),
        grid=(seq // rows,),
        out_specs=[pl.BlockSpec((rows, LANES), lambda i: (i, 0))] * 2,
        out_shape=[jax.ShapeDtypeStruct((seq, LANES), F32)] * 2,
        compiler_params=_cparams(("arbitrary",)),
        name="rope_tables",
    )()


def _norm_proj_kernel(x_ref, g_ref, w_ref, *rest, with_aux):
    if with_aux:
        waux_ref, o_ref, aux_ref, h_scr = rest
    else:
        o_ref, h_scr = rest
    j = pl.program_id(1)

    @pl.when(j == 0)
    def _():
        x = x_ref[...]
        ms = jnp.mean(x * x, axis=-1, keepdims=True)
        h = x * lax.rsqrt(ms + EPS) * g_ref[...]
        h_scr[...] = h.astype(BF16)
        if with_aux:
            aux_ref[...] = jnp.dot(h_scr[...], waux_ref[...], preferred_element_type=F32)

    o_ref[...] = jnp.dot(h_scr[...], w_ref[...], preferred_element_type=F32).astype(o_ref.dtype)


def _norm_proj(x, gain, w, w_aux=None, *, tm=1024, tn=1024):
    n, d = x.shape
    n_out = w.shape[1]
    tm = min(tm, n)
    tn = min(tn, n_out)
    with_aux = w_aux is not None
    in_specs = [
        pl.BlockSpec((tm, d), lambda i, j: (i, 0)),
        pl.BlockSpec((1, d), lambda i, j: (0, 0)),
        pl.BlockSpec((d, tn), lambda i, j: (0, j)),
    ]
    out_specs = [pl.BlockSpec((tm, tn), lambda i, j: (i, j))]
    out_shape = [jax.ShapeDtypeStruct((n, n_out), BF16)]
    args = [x, gain.reshape(1, d), w]
    if with_aux:
        in_specs.append(pl.BlockSpec((d, LANES), lambda i, j: (0, 0)))
        out_specs.append(pl.BlockSpec((tm, LANES), lambda i, j: (i, 0)))
        out_shape.append(jax.ShapeDtypeStruct((n, LANES), F32))
        args.append(w_aux)
    res = pl.pallas_call(
        functools.partial(_norm_proj_kernel, with_aux=with_aux),
        grid=(n // tm, n_out // tn),
        in_specs=in_specs,
        out_specs=out_specs,
        out_shape=out_shape,
        scratch_shapes=[pltpu.VMEM((tm, d), BF16)],
        compiler_params=_cparams(("parallel", "arbitrary")),
        name="norm_proj",
    )(*args)
    return res if with_aux else res[0]


def _rope128(x, cos, sin):
    return x * cos + pltpu.roll(x, RET_DK // 2, 1) * sin


def _retention_kernel(q_ref, k_ref, v_ref, g_ref, cos_ref, sin_ref, lg_ref, gn_ref, o_ref,
                      state, dmat, qdec, kdec):
    t = pl.program_id(2)
    tile = q_ref.shape[0]
    lg = lg_ref[0][:, :1]

    @pl.when(t == 0)
    def _():
        state[...] = jnp.zeros_like(state)
        ri = lax.broadcasted_iota(jnp.int32, (tile, tile), 0)
        ci = lax.broadcasted_iota(jnp.int32, (tile, tile), 1)
        causal = ri >= ci
        rel = jnp.where(causal, ri - ci, 0).astype(F32)
        dmat[...] = jnp.where(causal, jnp.exp(lg * rel), 0.0)
        pos = lax.broadcasted_iota(jnp.int32, (tile, RET_DK), 0).astype(F32)
        qdec[...] = jnp.exp(lg * (pos + 1.0))
        kdec[...] = jnp.exp(lg * (tile - 1.0 - pos))

    cos = cos_ref[...]
    sin = sin_ref[...]
    q = _rope128(q_ref[...].astype(F32), cos, sin)
    k = _rope128(k_ref[...].astype(F32), cos, sin) * (RET_DK ** -0.5)
    v = v_ref[...]
    s = _nt_dot(q.astype(BF16), k.astype(BF16))
    p = (s * dmat[...]).astype(BF16)
    o = jnp.dot(p, v, preferred_element_type=F32)
    st = state[...]
    o = o + jnp.dot((q * qdec[...]).astype(BF16), st.astype(BF16), preferred_element_type=F32)
    state[...] = st * jnp.exp(lg * float(tile)) + _tn_dot((k * kdec[...]).astype(BF16), v)
    ms = jnp.mean(o * o, axis=-1, keepdims=True)
    y = o * lax.rsqrt(ms + EPS) * gn_ref[...]
    gate = g_ref[...].astype(F32)
    o_ref[...] = (y * (gate * _sigmoid(gate))).astype(o_ref.dtype)


def _retention(proj, cos, sin, log_gamma, ret_norm, batch, seq):
    tile = min(RET_TILE, seq)
    nt = seq // tile
    qb = 0
    kb = RET_HEADS
    vb = (2 * RET_HEADS * RET_DK) // RET_DV
    gb = vb + RET_HEADS
    row = lambda b, h, t: b * nt + t
    return pl.pallas_call(
        _retention_kernel,
        grid=(batch, RET_HEADS, nt),
        in_specs=[
            pl.BlockSpec((tile, RET_DK), lambda b, h, t: (row(b, h, t), qb + h)),
            pl.BlockSpec((tile, RET_DK), lambda b, h, t: (row(b, h, t), kb + h)),
            pl.BlockSpec((tile, RET_DV), lambda b, h, t: (row(b, h, t), vb + h)),
            pl.BlockSpec((tile, RET_DV), lambda b, h, t: (row(b, h, t), gb + h)),
            pl.BlockSpec((tile, LANES), lambda b, h, t: (t, 0)),
            pl.BlockSpec((tile, LANES), lambda b, h, t: (t, 0)),
            pl.BlockSpec((1, 1, LANES), lambda b, h, t: (h, 0, 0)),
            pl.BlockSpec((1, RET_DV), lambda b, h, t: (0, 0)),
        ],
        out_specs=pl.BlockSpec((tile, RET_DV), lambda b, h, t: (row(b, h, t), h)),
        out_shape=jax.ShapeDtypeStruct((batch * seq, RET_HEADS * RET_DV), BF16),
        scratch_shapes=[
            pltpu.VMEM((RET_DK, RET_DV), F32),
            pltpu.VMEM((tile, tile), F32),
            pltpu.VMEM((tile, RET_DK), F32),
            pltpu.VMEM((tile, RET_DK), F32),
        ],
        compiler_params=_cparams(("parallel", "parallel", "arbitrary")),
        name="retention",
    )(proj, proj, proj, proj, cos, sin, log_gamma, ret_norm.reshape(1, RET_DV))


def _conv_silu(xbuf, x_ref, w_ref):
    tile = x_ref.shape[0]
    xbuf[SUBLANES:SUBLANES + tile, :] = x_ref[...].astype(F32)
    w = w_ref[...]
    base = SUBLANES - (CONV_WIDTH - 1)
    acc = xbuf[base:base + tile, :] * w[0:1, :]
    for j in range(1, CONV_WIDTH):
        acc = acc + xbuf[base + j:base + j + tile, :] * w[j:j + 1, :]
    xbuf[0:SUBLANES, :] = xbuf[tile:tile + SUBLANES, :]
    return acc * _sigmoid(acc)


def _l2norm(x):
    return x * lax.rsqrt(jnp.sum(x * x, axis=-1, keepdims=True) + EPS)


def _softplus(x):
    return jnp.maximum(x, 0.0) + jnp.log1p(jnp.exp(-jnp.abs(x)))


def _unit_lower_inverses(lows):
    c = lows[0].shape[0]
    dot = functools.partial(jnp.dot, preferred_element_type=F32)
    eye = (lax.broadcasted_iota(jnp.int32, (c, c), 0) == lax.broadcasted_iota(jnp.int32, (c, c), 1)).astype(F32)
    invs = [eye - low for low in lows]
    powers = [_split_bf16(low) for low in lows]
    span = 2
    while span < c:
        powers = [_split_bf16(_dot_split(hi, lo, hi, lo)) for hi, lo in powers]
        invs = [inv + _dot_split(*_split_bf16(inv), hi, lo) for inv, (hi, lo) in zip(invs, powers)]
        span *= 2
    return invs


def _split_bf16(a):
    hi = a.astype(BF16)
    return hi, (a - hi.astype(F32)).astype(BF16)


def _dot_split(a_hi, a_lo, b_hi, b_lo):
    dot = functools.partial(jnp.dot, preferred_element_type=F32)
    return dot(a_hi, b_hi) + dot(a_hi, b_lo) + dot(a_lo, b_hi)


def _chunk_cumsum(tril_b, g):
    dot = functools.partial(jnp.dot, preferred_element_type=F32)
    g1 = g.astype(BF16)
    r1 = g - g1.astype(F32)
    g2 = r1.astype(BF16)
    g3 = (r1 - g2.astype(F32)).astype(BF16)
    return dot(tril_b, g1) + dot(tril_b, g2) + dot(tril_b, g3)


def _gdn_kernel(xq_ref, xk_ref, xv_ref, z_ref, aux_ref, wq_ref, wk_ref, wv_ref, alog_ref, dtb_ref,
                gn_ref, o_ref, state, qbuf, kbuf, vbuf):
    hp = pl.program_id(1)
    t = pl.program_id(2)
    tile = xq_ref.shape[0]
    c = GDN_CHUNK
    hb = GDN_HEADS_PER_STEP

    @pl.when(t == 0)
    def _():
        state[...] = jnp.zeros_like(state)
        qbuf[0:SUBLANES, :] = jnp.zeros((SUBLANES, qbuf.shape[1]), F32)
        kbuf[0:SUBLANES, :] = jnp.zeros((SUBLANES, kbuf.shape[1]), F32)
        vbuf[0:SUBLANES, :] = jnp.zeros((SUBLANES, vbuf.shape[1]), F32)

    q_all = _conv_silu(qbuf, xq_ref, wq_ref)
    k_all = _conv_silu(kbuf, xk_ref, wk_ref)
    v_all = _conv_silu(vbuf, xv_ref, wv_ref)

    aux = aux_ref[...]
    lane = lax.broadcasted_iota(jnp.int32, (tile, LANES), 1)
    beta_all = _sigmoid(aux)
    g_all = -jnp.exp(alog_ref[...]) * _softplus(aux + dtb_ref[...])

    ri = lax.broadcasted_iota(jnp.int32, (c, c), 0)
    ci = lax.broadcasted_iota(jnp.int32, (c, c), 1)
    causal = ri >= ci
    strict = ri > ci
    tril_b = causal.astype(BF16)

    dot = functools.partial(jnp.dot, preferred_element_type=F32)
    nchunk = tile // c
    pairs = []
    for hh in range(hb):
        h = hp * hb + hh
        q = _l2norm(q_all[:, hh * GDN_DK:(hh + 1) * GDN_DK]) * (GDN_DK ** -0.5)
        k = _l2norm(k_all[:, hh * GDN_DK:(hh + 1) * GDN_DK])
        v = v_all[:, hh * GDN_DV:(hh + 1) * GDN_DV]
        beta = jnp.sum(jnp.where(lane == h, beta_all, 0.0), axis=1, keepdims=True)
        g = jnp.sum(jnp.where(lane == h + GDN_HEADS, g_all, 0.0), axis=1, keepdims=True)
        for i in range(nchunk):
            sl = slice(i * c, (i + 1) * c)
            qc, kc, vc, bc = q[sl], k[sl], v[sl], beta[sl]
            gcum = _chunk_cumsum(tril_b, jnp.broadcast_to(g[sl], (c, c)))
            rel = gcum - gcum.T
            decay = jnp.where(causal, jnp.exp(jnp.where(causal, rel, 0.0)), 0.0)
            eg = jnp.exp(gcum)
            kb = kc * bc
            kcb = kc.astype(BF16)
            g_last = gcum[c - 1:c, :]
            pairs.append(dict(
                hh=hh, i=i,
                low=jnp.where(strict, _nt_dot(kb.astype(BF16), kcb) * decay, 0.0),
                attn=jnp.where(causal, _nt_dot(qc.astype(BF16), kcb) * decay, 0.0).astype(BF16),
                vb=(vc * bc).astype(BF16),
                kbe=(kb * eg).astype(BF16),
                qg=(qc * eg).astype(BF16),
                k_tail_t=(kc * jnp.exp(g_last - gcum)).T.astype(BF16),
                sdec=jnp.exp(g_last[:, :1]),
            ))
    pairs.sort(key=lambda p: (p["i"], p["hh"]))
    invs = _unit_lower_inverses([p["low"] for p in pairs])
    for p, inv in zip(pairs, invs):
        inv_b = inv.astype(BF16)
        p["u"] = dot(inv_b, p["vb"])
        p["w"] = dot(inv_b, p["kbe"]).astype(BF16)
    for p in pairs:
        hh, sl = p["hh"], slice(p["i"] * c, (p["i"] + 1) * c)
        st = state[hh]
        stb = st.astype(BF16)
        v_new = (p["u"] - dot(p["w"], stb)).astype(BF16)
        state[hh] = st * p["sdec"] + dot(p["k_tail_t"], v_new)
        o = dot(p["qg"], stb) + dot(p["attn"], v_new)
        ms = jnp.mean(o * o, axis=-1, keepdims=True)
        y = o * lax.rsqrt(ms + EPS) * gn_ref[...]
        z = z_ref[sl, hh * GDN_DV:(hh + 1) * GDN_DV].astype(F32)
        o_ref[sl, hh * GDN_DV:(hh + 1) * GDN_DV] = (y * (z * _sigmoid(z))).astype(o_ref.dtype)


def _gdn(proj, aux, conv_w, alog_row, dtb_row, gdn_norm, batch, seq):
    tile = min(GDN_TILE, seq)
    nt = seq // tile
    hb = GDN_HEADS_PER_STEP
    dk, dv = hb * GDN_DK, hb * GDN_DV
    base = 2 * RET_HEADS * RET_DK + 2 * RET_HEADS * RET_DV
    qb = base // dk
    kb = qb + GDN_HEADS // hb
    vb = (base + 2 * GDN_HEADS * GDN_DK) // dv
    zb = vb + GDN_HEADS // hb
    cvb = (2 * GDN_HEADS * GDN_DK) // dv
    row = lambda b, h, t: b * nt + t
    return pl.pallas_call(
        _gdn_kernel,
        grid=(batch, GDN_HEADS // hb, nt),
        in_specs=[
            pl.BlockSpec((tile, dk), lambda b, h, t: (row(b, h, t), qb + h)),
            pl.BlockSpec((tile, dk), lambda b, h, t: (row(b, h, t), kb + h)),
            pl.BlockSpec((tile, dv), lambda b, h, t: (row(b, h, t), vb + h)),
            pl.BlockSpec((tile, dv), lambda b, h, t: (row(b, h, t), zb + h)),
            pl.BlockSpec((tile, LANES), lambda b, h, t: (row(b, h, t), 0)),
            pl.BlockSpec((CONV_WIDTH, dk), lambda b, h, t: (0, h)),
            pl.BlockSpec((CONV_WIDTH, dk), lambda b, h, t: (0, GDN_HEADS // hb + h)),
            pl.BlockSpec((CONV_WIDTH, dv), lambda b, h, t: (0, cvb + h)),
            pl.BlockSpec((1, LANES), lambda b, h, t: (0, 0)),
            pl.BlockSpec((1, LANES), lambda b, h, t: (0, 0)),
            pl.BlockSpec((1, GDN_DV), lambda b, h, t: (0, 0)),
        ],
        out_specs=pl.BlockSpec((tile, dv), lambda b, h, t: (row(b, h, t), h)),
        out_shape=jax.ShapeDtypeStruct((batch * seq, GDN_HEADS * GDN_DV), BF16),
        scratch_shapes=[
            pltpu.VMEM((hb, GDN_DK, GDN_DV), F32),
            pltpu.VMEM((tile + SUBLANES, dk), F32),
            pltpu.VMEM((tile + SUBLANES, dk), F32),
            pltpu.VMEM((tile + SUBLANES, dv), F32),
        ],
        compiler_params=_cparams(("parallel", "parallel", "arbitrary")),
        name="gated_deltanet",
    )(proj, proj, proj, proj, aux, conv_w, conv_w, conv_w, alog_row, dtb_row, gdn_norm.reshape(1, GDN_DV))


def _rope64(x, cos, sin):
    half = SWA_HEAD_DIM // 2
    lane = lax.broadcasted_iota(jnp.int32, x.shape, x.ndim - 1)
    first = (lane % SWA_HEAD_DIM) < half
    rot = jnp.where(first, pltpu.roll(x, LANES - half, x.ndim - 1), pltpu.roll(x, half, x.ndim - 1))
    return x * cos + rot * sin


def _head_rmsnorm(x, gain, seg_ones):
    sq_hi, sq_lo = _split_bf16(x * x)
    ss = jnp.dot(sq_hi, seg_ones, preferred_element_type=F32) + jnp.dot(sq_lo, seg_ones, preferred_element_type=F32)
    return x * lax.rsqrt(ss * (1.0 / SWA_HEAD_DIM) + EPS) * gain


def _swa_kernel(sink_ref, q_ref, k_ref, v_ref, cos_ref, sin_ref, qn_ref, kn_ref, seg_ref, o_ref,
                kprev, vprev):
    i = pl.program_id(1)
    w = WINDOW
    pairs = SWA_KV_HEADS // 2
    grp = SWA_Q_HEADS // SWA_KV_HEADS
    qchunks = grp * SWA_HEAD_DIM // LANES

    @pl.when(i == 0)
    def _():
        kprev[...] = jnp.zeros_like(kprev)
        vprev[...] = jnp.zeros_like(vprev)

    cos = cos_ref[...]
    sin = sin_ref[...]
    seg = seg_ref[...]
    lane = lax.broadcasted_iota(jnp.int32, (w, LANES), 1)
    lo_mask = lane < SWA_HEAD_DIM

    k_lo, k_hi, v_lo, v_hi = [], [], [], []
    for c in range(pairs):
        kc = k_ref[:, c * LANES:(c + 1) * LANES].astype(F32)
        kc = _rope64(_head_rmsnorm(kc, kn_ref[...], seg), cos, sin)
        vc = v_ref[:, c * LANES:(c + 1) * LANES].astype(F32)
        for src, lo_list, hi_list in ((kc, k_lo, k_hi), (vc, v_lo, v_hi)):
            a_lo = jnp.where(lo_mask, src, 0.0)
            b_hi = jnp.where(lo_mask, 0.0, src)
            lo_list += [a_lo, pltpu.roll(b_hi, SWA_HEAD_DIM, 1)]
            hi_list += [pltpu.roll(a_lo, SWA_HEAD_DIM, 1), b_hi]

    qpos = lax.broadcasted_iota(jnp.int32, (w, 2 * w), 0) + w
    kpos = lax.broadcasted_iota(jnp.int32, (w, 2 * w), 1)
    rel = qpos - kpos
    first_key = jnp.where(i > 0, 0, w)
    valid = (rel >= 0) & (rel < w) & (kpos >= first_key)
    scale = SWA_HEAD_DIM ** -0.5
    cos_q = jnp.concatenate([cos] * qchunks, axis=0)
    sin_q = jnp.concatenate([sin] * qchunks, axis=0)

    for kvh in range(SWA_KV_HEADS):
        q0 = kvh * grp * SWA_HEAD_DIM
        q2 = jnp.concatenate(
            [q_ref[:, q0 + c * LANES:q0 + (c + 1) * LANES].astype(F32) for c in range(qchunks)], axis=0)
        q2 = _head_rmsnorm(q2, qn_ref[...], seg)
        q2 = (_rope64(q2, cos_q, sin_q) * scale).astype(BF16)
        kl = jnp.concatenate([kprev[0, kvh], k_lo[kvh].astype(BF16)], axis=0)
        kh = jnp.concatenate([kprev[1, kvh], k_hi[kvh].astype(BF16)], axis=0)
        vl = jnp.concatenate([vprev[0, kvh], v_lo[kvh].astype(BF16)], axis=0)
        vh = jnp.concatenate([vprev[1, kvh], v_hi[kvh].astype(BF16)], axis=0)
        acc = None
        for par, kk, vv in ((0, kl, vl), (1, kh, vh)):
            sc = _nt_dot(q2, kk)
            probs = []
            for c in range(qchunks):
                sink = sink_ref[kvh * grp + 2 * c + par]
                s = jnp.where(valid, sc[c * w:(c + 1) * w, :], -jnp.inf)
                mx = jnp.maximum(jnp.max(s, axis=-1, keepdims=True), sink)
                p = jnp.exp(s - mx)
                denom = jnp.sum(p, axis=-1, keepdims=True) + jnp.exp(sink - mx)
                probs.append((p / denom).astype(BF16))
            pv = jnp.dot(jnp.concatenate(probs, axis=0), vv, preferred_element_type=F32)
            acc = pv if acc is None else acc + pv
        for c in range(qchunks):
            o_ref[:, q0 + c * LANES:q0 + (c + 1) * LANES] = acc[c * w:(c + 1) * w, :].astype(o_ref.dtype)

    for kvh in range(SWA_KV_HEADS):
        kprev[0, kvh] = k_lo[kvh].astype(BF16)
        kprev[1, kvh] = k_hi[kvh].astype(BF16)
        vprev[0, kvh] = v_lo[kvh].astype(BF16)
        vprev[1, kvh] = v_hi[kvh].astype(BF16)


def _swa(proj, cos, sin, q_norm, k_norm, sinks, batch, seq):
    w = WINDOW
    nb = seq // w
    qw = SWA_Q_HEADS * SWA_HEAD_DIM
    kvw = SWA_KV_HEADS * SWA_HEAD_DIM
    kblk = qw // kvw
    reps = LANES // SWA_HEAD_DIM
    seg = (jnp.arange(LANES)[:, None] // SWA_HEAD_DIM == jnp.arange(LANES)[None, :] // SWA_HEAD_DIM).astype(BF16)
    return pl.pallas_call(
        _swa_kernel,
        grid=(batch, nb),
        in_specs=[
            pl.BlockSpec(memory_space=pltpu.SMEM),
            pl.BlockSpec((w, qw), lambda b, i: (b * nb + i, 0)),
            pl.BlockSpec((w, kvw), lambda b, i: (b * nb + i, kblk)),
            pl.BlockSpec((w, kvw), lambda b, i: (b * nb + i, kblk + 1)),
            pl.BlockSpec((w, LANES), lambda b, i: (i, 0)),
            pl.BlockSpec((w, LANES), lambda b, i: (i, 0)),
            pl.BlockSpec((1, LANES), lambda b, i: (0, 0)),
            pl.BlockSpec((1, LANES), lambda b, i: (0, 0)),
            pl.BlockSpec((LANES, LANES), lambda b, i: (0, 0)),
        ],
        out_specs=pl.BlockSpec((w, qw), lambda b, i: (b * nb + i, 0)),
        out_shape=jax.ShapeDtypeStruct((batch * seq, qw), BF16),
        scratch_shapes=[
            pltpu.VMEM((2, SWA_KV_HEADS, w, LANES), BF16),
            pltpu.VMEM((2, SWA_KV_HEADS, w, LANES), BF16),
        ],
        compiler_params=_cparams(("parallel", "arbitrary")),
        name="swa",
    )(sinks.astype(F32), proj, proj, proj, cos, sin,
      jnp.tile(q_norm.astype(F32), reps).reshape(1, LANES),
      jnp.tile(k_norm.astype(F32), reps).reshape(1, LANES), seg)


def _route(logits):
    tm = logits.shape[0]
    lane = lax.broadcasted_iota(jnp.int32, (tm, LANES), 1)
    lane_f = lane.astype(F32)
    big = float(LANES)
    neg = -jnp.inf
    gl = jnp.where(lane < N_GROUPS, logits, neg)
    gmax = jnp.max(gl, axis=1, keepdims=True)
    gidx = jnp.min(jnp.where(gl == gmax, lane_f, big), axis=1, keepdims=True)
    g_p = 1.0 / jnp.sum(jnp.exp(gl - gmax), axis=1, keepdims=True)
    egroup = lax.shift_right_arithmetic(lane - N_GROUPS, 3).astype(F32)
    in_group = (lane >= N_GROUPS) & (lane < N_GROUPS + N_EXPERTS) & (egroup == gidx)
    el = jnp.where(in_group, logits, neg)
    emax = jnp.max(el, axis=1, keepdims=True)
    ee = jnp.exp(el - emax)
    prob = ee / jnp.sum(ee, axis=1, keepdims=True)
    pm = jnp.where(in_group, prob, -1.0)
    p1 = jnp.max(pm, axis=1, keepdims=True)
    i1 = jnp.min(jnp.where(pm == p1, lane_f, big), axis=1, keepdims=True)
    pm2 = jnp.where(lane_f == i1, -1.0, pm)
    p2 = jnp.max(pm2, axis=1, keepdims=True)
    i2 = jnp.min(jnp.where(pm2 == p2, lane_f, big), axis=1, keepdims=True)
    den = p1 + p2
    out = jnp.where(lane == 0, i1 - N_GROUPS, 0.0)
    out = jnp.where(lane == 1, i2 - N_GROUPS, out)
    out = jnp.where(lane == 2, g_p * p1 / den, out)
    out = jnp.where(lane == 3, g_p * p2 / den, out)
    return out


def _out_proj_kernel(*refs, n_act):
    acts = refs[:n_act]
    ws = refs[n_act:2 * n_act]
    x_ref, g_ref, wr_ref, x1_ref, h_ref, route_ref = refs[2 * n_act:]
    y = x_ref[...]
    for a, w in zip(acts, ws):
        y = y + jnp.dot(a[...], w[...], preferred_element_type=F32)
    x1_ref[...] = y
    ms = jnp.mean(y * y, axis=-1, keepdims=True)
    h = y * lax.rsqrt(ms + EPS) * g_ref[...]
    h_hi, h_lo = _split_bf16(h)
    h_ref[...] = h_hi
    route_ref[...] = _route(_dot_split(h_hi, h_lo, wr_ref[0], wr_ref[1]))


def _out_proj(acts, ws, x, gain, w_router, *, tm=256):
    n, d = x.shape
    tm = min(tm, n)
    n_act = len(acts)
    const = lambda i: (0, 0)
    in_specs = [pl.BlockSpec((tm, a.shape[1]), lambda i: (i, 0)) for a in acts]
    in_specs += [pl.BlockSpec(w.shape, const, pipeline_mode=pl.Buffered(1)) for w in ws]
    in_specs += [
        pl.BlockSpec((tm, d), lambda i: (i, 0)),
        pl.BlockSpec((1, d), const),
        pl.BlockSpec((2, d, LANES), lambda i: (0, 0, 0)),
    ]
    return pl.pallas_call(
        functools.partial(_out_proj_kernel, n_act=n_act),
        grid=(n // tm,),
        in_specs=in_specs,
        out_specs=[
            pl.BlockSpec((tm, d), lambda i: (i, 0)),
            pl.BlockSpec((tm, d), lambda i: (i, 0)),
            pl.BlockSpec((tm, LANES), lambda i: (i, 0)),
        ],
        out_shape=[
            jax.ShapeDtypeStruct((n, d), F32),
            jax.ShapeDtypeStruct((n, d), BF16),
            jax.ShapeDtypeStruct((n, LANES), F32),
        ],
        compiler_params=_cparams(("parallel",)),
        name="out_proj_router",
    )(*acts, *ws, x, gain.reshape(1, d), w_router)


def _expert_kernel(be_ref, nu_ref, x_ref, wg_ref, wu_ref, wd_ref, o_ref):
    i = pl.program_id(0)

    @pl.when(i < nu_ref[0])
    def _():
        x = x_ref[...]
        g = jnp.dot(x, wg_ref[0], preferred_element_type=F32)
        u = jnp.dot(x, wu_ref[0], preferred_element_type=F32)
        hid = (g * _sigmoid(g) * u).astype(BF16)
        o_ref[...] = jnp.dot(hid, wd_ref[0], preferred_element_type=F32).astype(o_ref.dtype)

    @pl.when(i >= nu_ref[0])
    def _():
        o_ref[...] = jnp.zeros_like(o_ref)


def _expert_mlp(xs, block_expert, n_used, w_gate, w_up, w_down):
    n_pad, d = xs.shape
    bm = MOE_BM
    ff = w_gate.shape[2]
    row = lambda i, be, nu: (jnp.minimum(i, nu[0] - 1), 0)
    grid_spec = pltpu.PrefetchScalarGridSpec(
        num_scalar_prefetch=2,
        grid=(n_pad // bm,),
        in_specs=[
            pl.BlockSpec((bm, d), row),
            pl.BlockSpec((1, d, ff), lambda i, be, nu: (be[i], 0, 0)),
            pl.BlockSpec((1, d, ff), lambda i, be, nu: (be[i], 0, 0)),
            pl.BlockSpec((1, ff, d), lambda i, be, nu: (be[i], 0, 0)),
        ],
        out_specs=pl.BlockSpec((bm, d), lambda i, be, nu: (i, 0)),
    )
    return pl.pallas_call(
        _expert_kernel,
        grid_spec=grid_spec,
        out_shape=jax.ShapeDtypeStruct((n_pad, d), BF16),
        compiler_params=_cparams(("arbitrary",)),
        name="expert_mlp",
    )(block_expert, n_used, xs, w_gate, w_up, w_down)


def _moe(x1, h2, route, w_gate, w_up, w_down):
    n, d = x1.shape
    bm = MOE_BM
    n_assign = 2 * n
    expert_id = route[:, 0:2].astype(jnp.int32).reshape(-1)
    gates = route[:, 2:4]
    onehot = (expert_id[:, None] == jnp.arange(N_EXPERTS, dtype=jnp.int32)[None, :]).astype(jnp.int32)
    csum = jnp.cumsum(onehot, axis=0)
    rank = jnp.sum(csum * onehot, axis=1) - 1
    counts = csum[-1]
    padded = (counts + bm - 1) // bm * bm
    pend = jnp.cumsum(padded)
    pstart = pend - padded
    dest = pstart[expert_id] + rank
    n_pad = (n_assign + N_EXPERTS * (bm - 1) + bm - 1) // bm * bm
    n_blk = n_pad // bm
    token_id = jnp.arange(n_assign, dtype=jnp.int32) // 2
    tok_pad = jnp.zeros((n_pad,), jnp.int32).at[dest].set(token_id)
    block_start = jnp.arange(n_blk, dtype=jnp.int32) * bm
    block_expert = jnp.minimum(jnp.sum(pend[None, :] <= block_start[:, None], axis=1), N_EXPERTS - 1)
    n_used = (pend[-1] // bm).astype(jnp.int32).reshape(1)
    xs = jnp.take(h2, tok_pad, axis=0)
    yb = _expert_mlp(xs, block_expert.astype(jnp.int32), n_used, w_gate, w_up, w_down)
    dest2 = dest.reshape(n, 2)
    y0 = jnp.take(yb, dest2[:, 0], axis=0).astype(F32)
    y1 = jnp.take(yb, dest2[:, 1], axis=0).astype(F32)
    return x1 + gates[:, 0:1] * y0 + gates[:, 1:2] * y1


def _router_weights(w_group, w_expert):
    d = w_group.shape[0]
    pad = jnp.zeros((d, LANES - N_GROUPS - N_EXPERTS), F32)
    w = jnp.concatenate([w_group.astype(F32), w_expert.astype(F32), pad], axis=1)
    w_hi = w.astype(BF16)
    w_lo = (w - w_hi.astype(F32)).astype(BF16)
    return jnp.stack([w_hi, w_lo])


def kernel(x, norm_mix, norm_ffn, even_w_in, ret_norm, gdn_conv, gdn_a_log, gdn_dt_bias, gdn_norm,
           even_w_out, odd_w_in, q_norm, k_norm, attn_sinks, odd_w_out, router_group, router_expert,
           expert_w_gate, expert_w_up, expert_w_down):
    batch, seq, d = x.shape
    n = batch * seq
    xt = x.reshape(n, d)

    w_in = even_w_in[0]
    w_main = w_in[:, :EVEN_MAIN].astype(BF16)
    w_aux = jnp.pad(w_in[:, EVEN_MAIN:], ((0, 0), (0, LANES - 2 * GDN_HEADS))).astype(BF16)
    proj, aux = _norm_proj(xt, norm_mix[0], w_main, w_aux)
    cos_r, sin_r = _rope_tables(seq, RET_DK // 2)
    log_gamma = jnp.log1p(-jnp.exp2(-5.0 - jnp.arange(RET_HEADS, dtype=F32)))
    log_gamma = jnp.broadcast_to(log_gamma[:, None, None], (RET_HEADS, 1, LANES))
    o_ret = _retention(proj, cos_r, sin_r, log_gamma, ret_norm[0].astype(F32), batch, seq)
    lane_pad = (GDN_HEADS, LANES - 2 * GDN_HEADS)
    alog_row = jnp.pad(gdn_a_log[0].astype(F32), lane_pad).reshape(1, LANES)
    dtb_row = jnp.pad(gdn_dt_bias[0].astype(F32), lane_pad).reshape(1, LANES)
    o_gdn = _gdn(proj, aux, gdn_conv[0].astype(F32), alog_row, dtb_row, gdn_norm[0].astype(F32), batch, seq)
    w_out = even_w_out[0].astype(BF16)
    split = RET_HEADS * RET_DV
    x1, h2, route = _out_proj([o_ret, o_gdn], [w_out[:split], w_out[split:]], xt, norm_ffn[0],
                              _router_weights(router_group[0], router_expert[0]))
    xt = _moe(x1, h2, route, expert_w_gate[0].astype(BF16), expert_w_up[0].astype(BF16),
              expert_w_down[0].astype(BF16))

    proj = _norm_proj(xt, norm_mix[1], odd_w_in[0].astype(BF16), tn=ODD_IN // 2)
    cos_s, sin_s = _rope_tables(seq, SWA_HEAD_DIM // 2)
    o_swa = _swa(proj, cos_s, sin_s, q_norm[0], k_norm[0], attn_sinks[0], batch, seq)
    x1, h2, route = _out_proj([o_swa], [odd_w_out[0].astype(BF16)], xt, norm_ffn[1],
                              _router_weights(router_group[1], router_expert[1]))
    xt = _moe(x1, h2, route, expert_w_gate[1].astype(BF16), expert_w_up[1].astype(BF16),
              expert_w_down[1].astype(BF16))
    return xt.reshape(batch, seq, d)
```

```python
import functools

import jax
import jax.numpy as jnp
from jax import lax
from jax.experimental import pallas as pl
from jax.experimental.pallas import tpu as pltpu

F32 = jnp.float32
BF16 = jnp.bfloat16

D_MODEL = 2048
RET_HEADS = 8
RET_DK = 128
RET_DV = 256
GDN_HEADS = 8
GDN_DK = 128
GDN_DV = 256
CONV_WIDTH = 4
SWA_Q_HEADS = 32
SWA_KV_HEADS = 4
SWA_HEAD_DIM = 64
WINDOW = 128
ROPE_THETA = 10000.0
N_GROUPS = 4
EXPERTS_PER_GROUP = 8
N_EXPERTS = N_GROUPS * EXPERTS_PER_GROUP
D_FF_EXPERT = 768
EPS = 1e-6

LANES = 128
SUBLANES = 8
VMEM_LIMIT = 56 * 1024 * 1024

EVEN_MAIN = 2 * RET_HEADS * RET_DK + 2 * RET_HEADS * RET_DV + 2 * GDN_HEADS * GDN_DK + 2 * GDN_HEADS * GDN_DV
ODD_IN = (SWA_Q_HEADS + 2 * SWA_KV_HEADS) * SWA_HEAD_DIM

RET_TILE = 256
GDN_TILE = 256
GDN_CHUNK = 128
GDN_HEADS_PER_STEP = 2
MOE_BM = 256
ROW_TILES = D_MODEL // LANES


def _cparams(sem):
    return pltpu.CompilerParams(dimension_semantics=sem, vmem_limit_bytes=VMEM_LIMIT)


def _sigmoid(x):
    return 1.0 / (1.0 + jnp.exp(-x))


def _nt_dot(a, b):
    return lax.dot_general(a, b, (((1,), (1,)), ((), ())), preferred_element_type=F32)


def _tn_dot(a, b):
    return lax.dot_general(a, b, (((0,), (0,)), ((), ())), preferred_element_type=F32)


def _rope_table_kernel(cos_ref, sin_ref, *, half):
    rows = cos_ref.shape[0]
    r0 = pl.program_id(0) * rows
    pos = (lax.broadcasted_iota(jnp.int32, (rows, LANES), 0) + r0).astype(F32)
    lane = lax.broadcasted_iota(jnp.int32, (rows, LANES), 1)
    fi = (lane % half).astype(F32)
    inv_freq = jnp.exp(-(fi / half) * jnp.log(ROPE_THETA))
    ang = pos * inv_freq
    first = (lane % (2 * half)) < half
    cos_ref[...] = jnp.cos(ang)
    sin_ref[...] = jnp.where(first, -jnp.sin(ang), jnp.sin(ang))


def _rope_tables(seq, half):
    rows = min(seq, 1024)
    return pl.pallas_call(
        functools.partial(_rope_table_kernel, half=half),
        grid=(seq // rows,),
        out_specs=[pl.BlockSpec((rows, LANES), lambda i: (i, 0))] * 2,
        out_shape=[jax.ShapeDtypeStruct((seq, LANES), F32)] * 2,
        compiler_params=_cparams(("arbitrary",)),
        name="rope_tables",
    )()


def _norm_proj_kernel(x_ref, g_ref, w_ref, *rest, with_aux):
    if with_aux:
        waux_ref, o_ref, aux_ref, h_scr = rest
    else:
        o_ref, h_scr = rest
    j = pl.program_id(1)

    @pl.when(j == 0)
    def _():
        x = x_ref[...]
        ms = jnp.mean(x * x, axis=-1, keepdims=True)
        h = x * lax.rsqrt(ms + EPS) * g_ref[...]
        h_scr[...] = h.astype(BF16)
        if with_aux:
            aux_ref[...] = jnp.dot(h_scr[...], waux_ref[...], preferred_element_type=F32)

    o_ref[...] = jnp.dot(h_scr[...], w_ref[...], preferred_element_type=F32).astype(o_ref.dtype)


def _norm_proj(x, gain, w, w_aux=None, *, tm=1024, tn=1024):
    n, d = x.shape
    n_out = w.shape[1]
    tm = min(tm, n)
    tn = min(tn, n_out)
    with_aux = w_aux is not None
    in_specs = [
        pl.BlockSpec((tm, d), lambda i, j: (i, 0)),
        pl.BlockSpec((1, d), lambda i, j: (0, 0)),
        pl.BlockSpec((d, tn), lambda i, j: (0, j)),
    ]
    out_specs = [pl.BlockSpec((tm, tn), lambda i, j: (i, j))]
    out_shape = [jax.ShapeDtypeStruct((n, n_out), BF16)]
    args = [x, gain.reshape(1, d), w]
    if with_aux:
        in_specs.append(pl.BlockSpec((d, LANES), lambda i, j: (0, 0)))
        out_specs.append(pl.BlockSpec((tm, LANES), lambda i, j: (i, 0)))
        out_shape.append(jax.ShapeDtypeStruct((n, LANES), F32))
        args.append(w_aux)
    res = pl.pallas_call(
        functools.partial(_norm_proj_kernel, with_aux=with_aux),
        grid=(n // tm, n_out // tn),
        in_specs=in_specs,
        out_specs=out_specs,
        out_shape=out_shape,
        scratch_shapes=[pltpu.VMEM((tm, d), BF16)],
        compiler_params=_cparams(("parallel", "arbitrary")),
        name="norm_proj",
    )(*args)
    return res if with_aux else res[0]


def _rope128(x, cos, sin):
    return x * cos + pltpu.roll(x, RET_DK // 2, 1) * sin


def _retention_kernel(q_ref, k_ref, v_ref, g_ref, cos_ref, sin_ref, lg_ref, gn_ref, o_ref,
                      state, dmat, qdec, kdec):
    t = pl.program_id(2)
    tile = q_ref.shape[0]
    lg = lg_ref[0][:, :1]

    @pl.when(t == 0)
    def _():
        state[...] = jnp.zeros_like(state)
        ri = lax.broadcasted_iota(jnp.int32, (tile, tile), 0)
        ci = lax.broadcasted_iota(jnp.int32, (tile, tile), 1)
        causal = ri >= ci
        rel = jnp.where(causal, ri - ci, 0).astype(F32)
        dmat[...] = jnp.where(causal, jnp.exp(lg * rel), 0.0)
        pos = lax.broadcasted_iota(jnp.int32, (tile, RET_DK), 0).astype(F32)
        qdec[...] = jnp.exp(lg * (pos + 1.0))
        kdec[...] = jnp.exp(lg * (tile - 1.0 - pos))

    cos = cos_ref[...]
    sin = sin_ref[...]
    q = _rope128(q_ref[...].astype(F32), cos, sin)
    k = _rope128(k_ref[...].astype(F32), cos, sin) * (RET_DK ** -0.5)
    v = v_ref[...]
    s = _nt_dot(q.astype(BF16), k.astype(BF16))
    p = (s * dmat[...]).astype(BF16)
    o = jnp.dot(p, v, preferred_element_type=F32)
    st = state[...]
    o = o + jnp.dot((q * qdec[...]).astype(BF16), st.astype(BF16), preferred_element_type=F32)
    state[...] = st * jnp.exp(lg * float(tile)) + _tn_dot((k * kdec[...]).astype(BF16), v)
    ms = jnp.mean(o * o, axis=-1, keepdims=True)
    y = o * lax.rsqrt(ms + EPS) * gn_ref[...]
    gate = g_ref[...].astype(F32)
    o_ref[...] = (y * (gate * _sigmoid(gate))).astype(o_ref.dtype)


def _retention(proj, cos, sin, log_gamma, ret_norm, batch, seq):
    tile = min(RET_TILE, seq)
    nt = seq // tile
    qb = 0
    kb = RET_HEADS
    vb = (2 * RET_HEADS * RET_DK) // RET_DV
    gb = vb + RET_HEADS
    row = lambda b, h, t: b * nt + t
    return pl.pallas_call(
        _retention_kernel,
        grid=(batch, RET_HEADS, nt),
        in_specs=[
            pl.BlockSpec((tile, RET_DK), lambda b, h, t: (row(b, h, t), qb + h)),
            pl.BlockSpec((tile, RET_DK), lambda b, h, t: (row(b, h, t), kb + h)),
            pl.BlockSpec((tile, RET_DV), lambda b, h, t: (row(b, h, t), vb + h)),
            pl.BlockSpec((tile, RET_DV), lambda b, h, t: (row(b, h, t), gb + h)),
            pl.BlockSpec((tile, LANES), lambda b, h, t: (t, 0)),
            pl.BlockSpec((tile, LANES), lambda b, h, t: (t, 0)),
            pl.BlockSpec((1, 1, LANES), lambda b, h, t: (h, 0, 0)),
            pl.BlockSpec((1, RET_DV), lambda b, h, t: (0, 0)),
        ],
        out_specs=pl.BlockSpec((tile, RET_DV), lambda b, h, t: (row(b, h, t), h)),
        out_shape=jax.ShapeDtypeStruct((batch * seq, RET_HEADS * RET_DV), BF16),
        scratch_shapes=[
            pltpu.VMEM((RET_DK, RET_DV), F32),
            pltpu.VMEM((tile, tile), F32),
            pltpu.VMEM((tile, RET_DK), F32),
            pltpu.VMEM((tile, RET_DK), F32),
        ],
        compiler_params=_cparams(("parallel", "parallel", "arbitrary")),
        name="retention",
    )(proj, proj, proj, proj, cos, sin, log_gamma, ret_norm.reshape(1, RET_DV))


def _conv_silu(xbuf, x_ref, w_ref):
    tile = x_ref.shape[0]
    xbuf[SUBLANES:SUBLANES + tile, :] = x_ref[...].astype(F32)
    w = w_ref[...]
    base = SUBLANES - (CONV_WIDTH - 1)
    acc = xbuf[base:base + tile, :] * w[0:1, :]
    for j in range(1, CONV_WIDTH):
        acc = acc + xbuf[base + j:base + j + tile, :] * w[j:j + 1, :]
    xbuf[0:SUBLANES, :] = xbuf[tile:tile + SUBLANES, :]
    return acc * _sigmoid(acc)


def _l2norm(x):
    return x * lax.rsqrt(jnp.sum(x * x, axis=-1, keepdims=True) + EPS)


def _softplus(x):
    return jnp.maximum(x, 0.0) + jnp.log1p(jnp.exp(-jnp.abs(x)))


def _unit_lower_inverses(lows):
    c = lows[0].shape[0]
    dot = functools.partial(jnp.dot, preferred_element_type=F32)
    eye = (lax.broadcasted_iota(jnp.int32, (c, c), 0) == lax.broadcasted_iota(jnp.int32, (c, c), 1)).astype(F32)
    invs = [eye - low for low in lows]
    powers = [_split_bf16(low) for low in lows]
    span = 2
    while span < c:
        powers = [_split_bf16(_dot_split(hi, lo, hi, lo)) for hi, lo in powers]
        invs = [inv + _dot_split(*_split_bf16(inv), hi, lo) for inv, (hi, lo) in zip(invs, powers)]
        span *= 2
    return invs


def _split_bf16(a):
    hi = a.astype(BF16)
    return hi, (a - hi.astype(F32)).astype(BF16)


def _dot_split(a_hi, a_lo, b_hi, b_lo):
    dot = functools.partial(jnp.dot, preferred_element_type=F32)
    return dot(a_hi, b_hi) + dot(a_hi, b_lo) + dot(a_lo, b_hi)


def _chunk_cumsum(tril_b, g):
    dot = functools.partial(jnp.dot, preferred_element_type=F32)
    g1 = g.astype(BF16)
    r1 = g - g1.astype(F32)
    g2 = r1.astype(BF16)
    g3 = (r1 - g2.astype(F32)).astype(BF16)
    return dot(tril_b, g1) + dot(tril_b, g2) + dot(tril_b, g3)


def _gdn_kernel(xq_ref, xk_ref, xv_ref, z_ref, aux_ref, wq_ref, wk_ref, wv_ref, alog_ref, dtb_ref,
                gn_ref, o_ref, state, qbuf, kbuf, vbuf):
    hp = pl.program_id(1)
    t = pl.program_id(2)
    tile = xq_ref.shape[0]
    c = GDN_CHUNK
    hb = GDN_HEADS_PER_STEP

    @pl.when(t == 0)
    def _():
        state[...] = jnp.zeros_like(state)
        qbuf[0:SUBLANES, :] = jnp.zeros((SUBLANES, qbuf.shape[1]), F32)
        kbuf[0:SUBLANES, :] = jnp.zeros((SUBLANES, kbuf.shape[1]), F32)
        vbuf[0:SUBLANES, :] = jnp.zeros((SUBLANES, vbuf.shape[1]), F32)

    q_all = _conv_silu(qbuf, xq_ref, wq_ref)
    k_all = _conv_silu(kbuf, xk_ref, wk_ref)
    v_all = _conv_silu(vbuf, xv_ref, wv_ref)

    aux = aux_ref[...]
    lane = lax.broadcasted_iota(jnp.int32, (tile, LANES), 1)
    beta_all = _sigmoid(aux)
    g_all = -jnp.exp(alog_ref[...]) * _softplus(aux + dtb_ref[...])

    ri = lax.broadcasted_iota(jnp.int32, (c, c), 0)
    ci = lax.broadcasted_iota(jnp.int32, (c, c), 1)
    causal = ri >= ci
    strict = ri > ci
    tril_b = causal.astype(BF16)

    dot = functools.partial(jnp.dot, preferred_element_type=F32)
    nchunk = tile // c
    pairs = []
    for hh in range(hb):
        h = hp * hb + hh
        q = _l2norm(q_all[:, hh * GDN_DK:(hh + 1) * GDN_DK]) * (GDN_DK ** -0.5)
        k = _l2norm(k_all[:, hh * GDN_DK:(hh + 1) * GDN_DK])
        v = v_all[:, hh * GDN_DV:(hh + 1) * GDN_DV]
        beta = jnp.sum(jnp.where(lane == h, beta_all, 0.0), axis=1, keepdims=True)
        g = jnp.sum(jnp.where(lane == h + GDN_HEADS, g_all, 0.0), axis=1, keepdims=True)
        for i in range(nchunk):
            sl = slice(i * c, (i + 1) * c)
            qc, kc, vc, bc = q[sl], k[sl], v[sl], beta[sl]
            gcum = _chunk_cumsum(tril_b, jnp.broadcast_to(g[sl], (c, c)))
            rel = gcum - gcum.T
            decay = jnp.where(causal, jnp.exp(jnp.where(causal, rel, 0.0)), 0.0)
            eg = jnp.exp(gcum)
            kb = kc * bc
            kcb = kc.astype(BF16)
            g_last = gcum[c - 1:c, :]
            pairs.append(dict(
                hh=hh, i=i,
                low=jnp.where(strict, _nt_dot(kb.astype(BF16), kcb) * decay, 0.0),
                attn=jnp.where(causal, _nt_dot(qc.astype(BF16), kcb) * decay, 0.0).astype(BF16),
                vb=(vc * bc).astype(BF16),
                kbe=(kb * eg).astype(BF16),
                qg=(qc * eg).astype(BF16),
                k_tail_t=(kc * jnp.exp(g_last - gcum)).T.astype(BF16),
                sdec=jnp.exp(g_last[:, :1]),
            ))
    pairs.sort(key=lambda p: (p["i"], p["hh"]))
    invs = _unit_lower_inverses([p["low"] for p in pairs])
    for p, inv in zip(pairs, invs):
        inv_b = inv.astype(BF16)
        p["u"] = dot(inv_b, p["vb"])
        p["w"] = dot(inv_b, p["kbe"]).astype(BF16)
    for p in pairs:
        hh, sl = p["hh"], slice(p["i"] * c, (p["i"] + 1) * c)
        st = state[hh]
        stb = st.astype(BF16)
        v_new = (p["u"] - dot(p["w"], stb)).astype(BF16)
        state[hh] = st * p["sdec"] + dot(p["k_tail_t"], v_new)
        o = dot(p["qg"], stb) + dot(p["attn"], v_new)
        ms = jnp.mean(o * o, axis=-1, keepdims=True)
        y = o * lax.rsqrt(ms + EPS) * gn_ref[...]
        z = z_ref[sl, hh * GDN_DV:(hh + 1) * GDN_DV].astype(F32)
        o_ref[sl, hh * GDN_DV:(hh + 1) * GDN_DV] = (y * (z * _sigmoid(z))).astype(o_ref.dtype)


def _gdn(proj, aux, conv_w, alog_row, dtb_row, gdn_norm, batch, seq):
    tile = min(GDN_TILE, seq)
    nt = seq // tile
    hb = GDN_HEADS_PER_STEP
    dk, dv = hb * GDN_DK, hb * GDN_DV
    base = 2 * RET_HEADS * RET_DK + 2 * RET_HEADS * RET_DV
    qb = base // dk
    kb = qb + GDN_HEADS // hb
    vb = (base + 2 * GDN_HEADS * GDN_DK) // dv
    zb = vb + GDN_HEADS // hb
    cvb = (2 * GDN_HEADS * GDN_DK) // dv
    row = lambda b, h, t: b * nt + t
    return pl.pallas_call(
        _gdn_kernel,
        grid=(batch, GDN_HEADS // hb, nt),
        in_specs=[
            pl.BlockSpec((tile, dk), lambda b, h, t: (row(b, h, t), qb + h)),
            pl.BlockSpec((tile, dk), lambda b, h, t: (row(b, h, t), kb + h)),
            pl.BlockSpec((tile, dv), lambda b, h, t: (row(b, h, t), vb + h)),
            pl.BlockSpec((tile, dv), lambda b, h, t: (row(b, h, t), zb + h)),
            pl.BlockSpec((tile, LANES), lambda b, h, t: (row(b, h, t), 0)),
            pl.BlockSpec((CONV_WIDTH, dk), lambda b, h, t: (0, h)),
            pl.BlockSpec((CONV_WIDTH, dk), lambda b, h, t: (0, GDN_HEADS // hb + h)),
            pl.BlockSpec((CONV_WIDTH, dv), lambda b, h, t: (0, cvb + h)),
            pl.BlockSpec((1, LANES), lambda b, h, t: (0, 0)),
            pl.BlockSpec((1, LANES), lambda b, h, t: (0, 0)),
            pl.BlockSpec((1, GDN_DV), lambda b, h, t: (0, 0)),
        ],
        out_specs=pl.BlockSpec((tile, dv), lambda b, h, t: (row(b, h, t), h)),
        out_shape=jax.ShapeDtypeStruct((batch * seq, GDN_HEADS * GDN_DV), BF16),
        scratch_shapes=[
            pltpu.VMEM((hb, GDN_DK, GDN_DV), F32),
            pltpu.VMEM((tile + SUBLANES, dk), F32),
            pltpu.VMEM((tile + SUBLANES, dk), F32),
            pltpu.VMEM((tile + SUBLANES, dv), F32),
        ],
        compiler_params=_cparams(("parallel", "parallel", "arbitrary")),
        name="gated_deltanet",
    )(proj, proj, proj, proj, aux, conv_w, conv_w, conv_w, alog_row, dtb_row, gdn_norm.reshape(1, GDN_DV))


def _rope64(x, cos, sin):
    half = SWA_HEAD_DIM // 2
    lane = lax.broadcasted_iota(jnp.int32, x.shape, x.ndim - 1)
    first = (lane % SWA_HEAD_DIM) < half
    rot = jnp.where(first, pltpu.roll(x, LANES - half, x.ndim - 1), pltpu.roll(x, half, x.ndim - 1))
    return x * cos + rot * sin


def _head_rmsnorm(x, gain, seg_ones):
    sq_hi, sq_lo = _split_bf16(x * x)
    ss = jnp.dot(sq_hi, seg_ones, preferred_element_type=F32) + jnp.dot(sq_lo, seg_ones, preferred_element_type=F32)
    return x * lax.rsqrt(ss * (1.0 / SWA_HEAD_DIM) + EPS) * gain


def _swa_kernel(sink_ref, q_ref, k_ref, v_ref, cos_ref, sin_ref, qn_ref, kn_ref, seg_ref, o_ref,
                kprev, vprev):
    i = pl.program_id(1)
    w = WINDOW
    pairs = SWA_KV_HEADS // 2
    grp = SWA_Q_HEADS // SWA_KV_HEADS
    qchunks = grp * SWA_HEAD_DIM // LANES

    @pl.when(i == 0)
    def _():
        kprev[...] = jnp.zeros_like(kprev)
        vprev[...] = jnp.zeros_like(vprev)

    cos = cos_ref[...]
    sin = sin_ref[...]
    seg = seg_ref[...]
    lane = lax.broadcasted_iota(jnp.int32, (w, LANES), 1)
    lo_mask = lane < SWA_HEAD_DIM

    k_lo, k_hi, v_lo, v_hi = [], [], [], []
    for c in range(pairs):
        kc = k_ref[:, c * LANES:(c + 1) * LANES].astype(F32)
        kc = _rope64(_head_rmsnorm(kc, kn_ref[...], seg), cos, sin)
        vc = v_ref[:, c * LANES:(c + 1) * LANES].astype(F32)
        for src, lo_list, hi_list in ((kc, k_lo, k_hi), (vc, v_lo, v_hi)):
            a_lo = jnp.where(lo_mask, src, 0.0)
            b_hi = jnp.where(lo_mask, 0.0, src)
            lo_list += [a_lo, pltpu.roll(b_hi, SWA_HEAD_DIM, 1)]
            hi_list += [pltpu.roll(a_lo, SWA_HEAD_DIM, 1), b_hi]

    qpos = lax.broadcasted_iota(jnp.int32, (w, 2 * w), 0) + w
    kpos = lax.broadcasted_iota(jnp.int32, (w, 2 * w), 1)
    rel = qpos - kpos
    first_key = jnp.where(i > 0, 0, w)
    valid = (rel >= 0) & (rel < w) & (kpos >= first_key)
    scale = SWA_HEAD_DIM ** -0.5
    cos_q = jnp.concatenate([cos] * qchunks, axis=0)
    sin_q = jnp.concatenate([sin] * qchunks, axis=0)

    for kvh in range(SWA_KV_HEADS):
        q0 = kvh * grp * SWA_HEAD_DIM
        q2 = jnp.concatenate(
            [q_ref[:, q0 + c * LANES:q0 + (c + 1) * LANES].astype(F32) for c in range(qchunks)], axis=0)
        q2 = _head_rmsnorm(q2, qn_ref[...], seg)
        q2 = (_rope64(q2, cos_q, sin_q) * scale).astype(BF16)
        kl = jnp.concatenate([kprev[0, kvh], k_lo[kvh].astype(BF16)], axis=0)
        kh = jnp.concatenate([kprev[1, kvh], k_hi[kvh].astype(BF16)], axis=0)
        vl = jnp.concatenate([vprev[0, kvh], v_lo[kvh].astype(BF16)], axis=0)
        vh = jnp.concatenate([vprev[1, kvh], v_hi[kvh].astype(BF16)], axis=0)
        acc = None
        for par, kk, vv in ((0, kl, vl), (1, kh, vh)):
            sc = _nt_dot(q2, kk)
            probs = []
            for c in range(qchunks):
                sink = sink_ref[kvh * grp + 2 * c + par]
                s = jnp.where(valid, sc[c * w:(c + 1) * w, :], -jnp.inf)
                mx = jnp.maximum(jnp.max(s, axis=-1, keepdims=True), sink)
                p = jnp.exp(s - mx)
                denom = jnp.sum(p, axis=-1, keepdims=True) + jnp.exp(sink - mx)
                probs.append((p / denom).astype(BF16))
            pv = jnp.dot(jnp.concatenate(probs, axis=0), vv, preferred_element_type=F32)
            acc = pv if acc is None else acc + pv
        for c in range(qchunks):
            o_ref[:, q0 + c * LANES:q0 + (c + 1) * LANES] = acc[c * w:(c + 1) * w, :].astype(o_ref.dtype)

    for kvh in range(SWA_KV_HEADS):
        kprev[0, kvh] = k_lo[kvh].astype(BF16)
        kprev[1, kvh] = k_hi[kvh].astype(BF16)
        vprev[0, kvh] = v_lo[kvh].astype(BF16)
        vprev[1, kvh] = v_hi[kvh].astype(BF16)


def _swa(proj, cos, sin, q_norm, k_norm, sinks, batch, seq):
    w = WINDOW
    nb = seq // w
    qw = SWA_Q_HEADS * SWA_HEAD_DIM
    kvw = SWA_KV_HEADS * SWA_HEAD_DIM
    kblk = qw // kvw
    reps = LANES // SWA_HEAD_DIM
    seg = (jnp.arange(LANES)[:, None] // SWA_HEAD_DIM == jnp.arange(LANES)[None, :] // SWA_HEAD_DIM).astype(BF16)
    return pl.pallas_call(
        _swa_kernel,
        grid=(batch, nb),
        in_specs=[
            pl.BlockSpec(memory_space=pltpu.SMEM),
            pl.BlockSpec((w, qw), lambda b, i: (b * nb + i, 0)),
            pl.BlockSpec((w, kvw), lambda b, i: (b * nb + i, kblk)),
            pl.BlockSpec((w, kvw), lambda b, i: (b * nb + i, kblk + 1)),
            pl.BlockSpec((w, LANES), lambda b, i: (i, 0)),
            pl.BlockSpec((w, LANES), lambda b, i: (i, 0)),
            pl.BlockSpec((1, LANES), lambda b, i: (0, 0)),
            pl.BlockSpec((1, LANES), lambda b, i: (0, 0)),
            pl.BlockSpec((LANES, LANES), lambda b, i: (0, 0)),
        ],
        out_specs=pl.BlockSpec((w, qw), lambda b, i: (b * nb + i, 0)),
        out_shape=jax.ShapeDtypeStruct((batch * seq, qw), BF16),
        scratch_shapes=[
            pltpu.VMEM((2, SWA_KV_HEADS, w, LANES), BF16),
            pltpu.VMEM((2, SWA_KV_HEADS, w, LANES), BF16),
        ],
        compiler_params=_cparams(("parallel", "arbitrary")),
        name="swa",
    )(sinks.astype(F32), proj, proj, proj, cos, sin,
      jnp.tile(q_norm.astype(F32), reps).reshape(1, LANES),
      jnp.tile(k_norm.astype(F32), reps).reshape(1, LANES), seg)


def _route(logits):
    tm = logits.shape[0]
    lane = lax.broadcasted_iota(jnp.int32, (tm, LANES), 1)
    lane_f = lane.astype(F32)
    big = float(LANES)
    neg = -jnp.inf
    gl = jnp.where(lane < N_GROUPS, logits, neg)
    gmax = jnp.max(gl, axis=1, keepdims=True)
    gidx = jnp.min(jnp.where(gl == gmax, lane_f, big), axis=1, keepdims=True)
    g_p = 1.0 / jnp.sum(jnp.exp(gl - gmax), axis=1, keepdims=True)
    egroup = lax.shift_right_arithmetic(lane - N_GROUPS, 3).astype(F32)
    in_group = (lane >= N_GROUPS) & (lane < N_GROUPS + N_EXPERTS) & (egroup == gidx)
    el = jnp.where(in_group, logits, neg)
    emax = jnp.max(el, axis=1, keepdims=True)
    ee = jnp.exp(el - emax)
    prob = ee / jnp.sum(ee, axis=1, keepdims=True)
    pm = jnp.where(in_group, prob, -1.0)
    p1 = jnp.max(pm, axis=1, keepdims=True)
    i1 = jnp.min(jnp.where(pm == p1, lane_f, big), axis=1, keepdims=True)
    pm2 = jnp.where(lane_f == i1, -1.0, pm)
    p2 = jnp.max(pm2, axis=1, keepdims=True)
    i2 = jnp.min(jnp.where(pm2 == p2, lane_f, big), axis=1, keepdims=True)
    den = p1 + p2
    out = jnp.where(lane == 0, i1 - N_GROUPS, 0.0)
    out = jnp.where(lane == 1, i2 - N_GROUPS, out)
    out = jnp.where(lane == 2, g_p * p1 / den, out)
    out = jnp.where(lane == 3, g_p * p2 / den, out)
    return out


def _out_proj_kernel(*refs, n_act):
    acts = refs[:n_act]
    ws = refs[n_act:2 * n_act]
    x_ref, g_ref, wr_ref, x1_ref, h_ref, route_ref = refs[2 * n_act:]
    y = x_ref[...]
    for a, w in zip(acts, ws):
        y = y + jnp.dot(a[...], w[...], preferred_element_type=F32)
    x1_ref[...] = y
    ms = jnp.mean(y * y, axis=-1, keepdims=True)
    h = y * lax.rsqrt(ms + EPS) * g_ref[...]
    tm = y.shape[0]
    for s in range(ROW_TILES):
        h_ref[pl.ds(s, tm, stride=ROW_TILES), :] = h[:, s * LANES:(s + 1) * LANES]
    h_hi, h_lo = _split_bf16(h)
    route_ref[...] = _route(_dot_split(h_hi, h_lo, wr_ref[0], wr_ref[1]))


def _out_proj(acts, ws, x, gain, w_router, *, tm=256):
    n, d = x.shape
    tm = min(tm, n)
    n_act = len(acts)
    const = lambda i: (0, 0)
    in_specs = [pl.BlockSpec((tm, a.shape[1]), lambda i: (i, 0)) for a in acts]
    in_specs += [pl.BlockSpec(w.shape, const, pipeline_mode=pl.Buffered(1)) for w in ws]
    in_specs += [
        pl.BlockSpec((tm, d), lambda i: (i, 0)),
        pl.BlockSpec((1, d), const),
        pl.BlockSpec((2, d, LANES), lambda i: (0, 0, 0)),
    ]
    return pl.pallas_call(
        functools.partial(_out_proj_kernel, n_act=n_act),
        grid=(n // tm,),
        in_specs=in_specs,
        out_specs=[
            pl.BlockSpec((tm, d), lambda i: (i, 0)),
            pl.BlockSpec((tm * ROW_TILES, LANES), lambda i: (i, 0)),
            pl.BlockSpec((tm, LANES), lambda i: (i, 0)),
        ],
        out_shape=[
            jax.ShapeDtypeStruct((n, d), F32),
            jax.ShapeDtypeStruct((n * ROW_TILES, LANES), F32),
            jax.ShapeDtypeStruct((n, LANES), F32),
        ],
        compiler_params=_cparams(("parallel",)),
        name="out_proj_router",
    )(*acts, *ws, x, gain.reshape(1, d), w_router)


def _row_gather_start(idx_ref, base, count, src_hbm, dst, sem):
    def body(r, carry):
        tok = idx_ref[base + r]
        pltpu.make_async_copy(src_hbm.at[pl.ds(pl.multiple_of(tok * ROW_TILES, ROW_TILES), ROW_TILES), :],
                              dst.at[pl.ds(pl.multiple_of(r * ROW_TILES, ROW_TILES), ROW_TILES), :], sem).start()
        return carry
    lax.fori_loop(0, count, body, 0, unroll=8)


def _row_gather_wait(dst, sem):
    pltpu.make_async_copy(dst, dst, sem).wait()


def _untile_rows(buf, rows):
    return [buf[pl.ds(s, rows, stride=ROW_TILES), :] for s in range(ROW_TILES)]


def _expert_kernel(be_ref, nu_ref, nxt_ref, tok_ref, h_hbm, wg_hbm, wu_hbm, wd_hbm, o_ref,
                   xbuf, stage_g, stage_u, stage_d, wg_b, wu_b, wd_b, gsem, wsem):
    i = pl.program_id(0)
    bm = MOE_BM
    nu = nu_ref[0]
    slot = i % 2
    e = be_ref[i]

    def weight_copies(expert):
        return (pltpu.make_async_copy(wg_hbm.at[expert], stage_g, wsem.at[0]),
                pltpu.make_async_copy(wu_hbm.at[expert], stage_u, wsem.at[1]),
                pltpu.make_async_copy(wd_hbm.at[expert], stage_d, wsem.at[2]))

    @pl.when(i == 0)
    def _():
        _row_gather_start(tok_ref, 0, bm, h_hbm, xbuf.at[0], gsem.at[0])
        for cp in weight_copies(e):
            cp.start()

    @pl.when(i < nu)
    def _():
        @pl.when(i + 1 < nu)
        def _():
            _row_gather_start(tok_ref, (i + 1) * bm, bm, h_hbm, xbuf.at[1 - slot], gsem.at[1 - slot])

        first_of_run = jnp.logical_or(i == 0, e != be_ref[jnp.maximum(i - 1, 0)])

        @pl.when(first_of_run)
        def _():
            for cp, stage, dst in zip(weight_copies(e), (stage_g, stage_u, stage_d), (wg_b, wu_b, wd_b)):
                cp.wait()
                dst[...] = stage[...].astype(BF16)
            nxt = nxt_ref[e]

            @pl.when(nxt >= 0)
            def _():
                for cp in weight_copies(nxt):
                    cp.start()

        _row_gather_wait(xbuf.at[slot], gsem.at[slot])
        x = jnp.concatenate([c.astype(BF16) for c in _untile_rows(xbuf.at[slot], bm)], axis=1)
        g = jnp.dot(x, wg_b[...], preferred_element_type=F32)
        u = jnp.dot(x, wu_b[...], preferred_element_type=F32)
        hid = (g * _sigmoid(g) * u).astype(BF16)
        y = jnp.dot(hid, wd_b[...], preferred_element_type=F32)
        for s in range(ROW_TILES):
            o_ref[pl.ds(s, bm, stride=ROW_TILES), :] = y[:, s * LANES:(s + 1) * LANES]

    @pl.when(i >= nu)
    def _():
        o_ref[...] = jnp.zeros_like(o_ref)


def _expert_mlp(h_tiled, tok_pad, block_expert, n_used, next_expert, w_gate, w_up, w_down):
    n_pad = tok_pad.shape[0]
    bm = MOE_BM
    _, d, ff = w_gate.shape
    any_spec = pl.BlockSpec(memory_space=pl.ANY)
    grid_spec = pltpu.PrefetchScalarGridSpec(
        num_scalar_prefetch=4,
        grid=(n_pad // bm,),
        in_specs=[any_spec, any_spec, any_spec, any_spec],
        out_specs=pl.BlockSpec((bm * ROW_TILES, LANES), lambda i, *_: (i, 0)),
        scratch_shapes=[
            pltpu.VMEM((2, bm * ROW_TILES, LANES), F32),
            pltpu.VMEM((d, ff), F32), pltpu.VMEM((d, ff), F32), pltpu.VMEM((ff, d), F32),
            pltpu.VMEM((d, ff), BF16), pltpu.VMEM((d, ff), BF16), pltpu.VMEM((ff, d), BF16),
            pltpu.SemaphoreType.DMA((2,)),
            pltpu.SemaphoreType.DMA((3,)),
        ],
    )
    return pl.pallas_call(
        _expert_kernel,
        grid_spec=grid_spec,
        out_shape=jax.ShapeDtypeStruct((n_pad * ROW_TILES, LANES), F32),
        compiler_params=_cparams(("arbitrary",)),
        name="expert_mlp",
    )(block_expert, n_used, next_expert, tok_pad, h_tiled, w_gate, w_up, w_down)


def _combine_kernel(dest_ref, x1_ref, route_ref, y_hbm, o_ref, ybuf, sem):
    i = pl.program_id(0)
    nsteps = pl.num_programs(0)
    tm = x1_ref.shape[0]
    slot = i % 2

    def start(step, sl):
        for k in range(2):
            _row_gather_start(dest_ref, k * (nsteps * tm) + step * tm, tm, y_hbm, ybuf.at[sl, k], sem.at[sl])

    @pl.when(i == 0)
    def _():
        start(0, 0)

    @pl.when(i + 1 < nsteps)
    def _():
        start(i + 1, 1 - slot)

    _row_gather_wait(ybuf.at[slot], sem.at[slot])
    route = route_ref[...]
    g0 = route[:, 2:3]
    g1 = route[:, 3:4]
    for s in range(ROW_TILES):
        y0 = ybuf[slot, 0, pl.ds(s, tm, stride=ROW_TILES), :]
        y1 = ybuf[slot, 1, pl.ds(s, tm, stride=ROW_TILES), :]
        o_ref[:, s * LANES:(s + 1) * LANES] = x1_ref[:, s * LANES:(s + 1) * LANES] + g0 * y0 + g1 * y1


def _combine(x1, route, dest, y_tiled, *, tm=256):
    n, d = x1.shape
    tm = min(tm, n)
    grid_spec = pltpu.PrefetchScalarGridSpec(
        num_scalar_prefetch=1,
        grid=(n // tm,),
        in_specs=[
            pl.BlockSpec((tm, d), lambda i, *_: (i, 0)),
            pl.BlockSpec((tm, LANES), lambda i, *_: (i, 0)),
            pl.BlockSpec(memory_space=pl.ANY),
        ],
        out_specs=pl.BlockSpec((tm, d), lambda i, *_: (i, 0)),
        scratch_shapes=[
            pltpu.VMEM((2, 2, tm * ROW_TILES, LANES), F32),
            pltpu.SemaphoreType.DMA((2,)),
        ],
    )
    return pl.pallas_call(
        _combine_kernel,
        grid_spec=grid_spec,
        out_shape=jax.ShapeDtypeStruct((n, d), F32),
        compiler_params=_cparams(("arbitrary",)),
        name="moe_combine",
    )(dest, x1, route, y_tiled)


def _moe(x1, h_tiled, route, w_gate, w_up, w_down):
    n, d = x1.shape
    bm = MOE_BM
    n_assign = 2 * n
    expert_id = route[:, 0:2].astype(jnp.int32).reshape(-1)
    onehot = (expert_id[:, None] == jnp.arange(N_EXPERTS, dtype=jnp.int32)[None, :]).astype(jnp.int32)
    csum = jnp.cumsum(onehot, axis=0)
    rank = jnp.sum(csum * onehot, axis=1) - 1
    counts = csum[-1]
    padded = (counts + bm - 1) // bm * bm
    pend = jnp.cumsum(padded)
    pstart = pend - padded
    dest = (jnp.sum(onehot * pstart[None, :], axis=1) + rank).astype(jnp.int32)
    n_pad = (n_assign + N_EXPERTS * (bm - 1) + bm - 1) // bm * bm
    n_blk = n_pad // bm
    token_id = jnp.arange(n_assign, dtype=jnp.int32) // 2
    tok_pad = (jnp.arange(n_pad, dtype=jnp.int32) % n).at[dest].set(token_id)
    block_start = jnp.arange(n_blk, dtype=jnp.int32) * bm
    block_expert = jnp.minimum(jnp.sum(pend[None, :] <= block_start[:, None], axis=1), N_EXPERTS - 1)
    n_used = (pend[-1] // bm).astype(jnp.int32).reshape(1)
    ids = jnp.arange(N_EXPERTS, dtype=jnp.int32)
    cand = jnp.where((counts[None, :] > 0) & (ids[None, :] > ids[:, None]), ids[None, :], N_EXPERTS)
    next_expert = jnp.min(cand, axis=1)
    next_expert = jnp.where(next_expert >= N_EXPERTS, -1, next_expert).astype(jnp.int32)
    y_tiled = _expert_mlp(h_tiled, tok_pad, block_expert.astype(jnp.int32), n_used, next_expert,
                          w_gate, w_up, w_down)
    dest_by_choice = dest.reshape(n, 2).T.reshape(-1).astype(jnp.int32)
    return _combine(x1, route, dest_by_choice, y_tiled)


def _router_weights(w_group, w_expert):
    d = w_group.shape[0]
    pad = jnp.zeros((d, LANES - N_GROUPS - N_EXPERTS), F32)
    w = jnp.concatenate([w_group.astype(F32), w_expert.astype(F32), pad], axis=1)
    w_hi = w.astype(BF16)
    w_lo = (w - w_hi.astype(F32)).astype(BF16)
    return jnp.stack([w_hi, w_lo])


def kernel(x, norm_mix, norm_ffn, even_w_in, ret_norm, gdn_conv, gdn_a_log, gdn_dt_bias, gdn_norm,
           even_w_out, odd_w_in, q_norm, k_norm, attn_sinks, odd_w_out, router_group, router_expert,
           expert_w_gate, expert_w_up, expert_w_down):
    batch, seq, d = x.shape
    n = batch * seq
    xt = x.reshape(n, d)

    w_in = even_w_in[0]
    w_main = w_in[:, :EVEN_MAIN].astype(BF16)
    w_aux = jnp.pad(w_in[:, EVEN_MAIN:], ((0, 0), (0, LANES - 2 * GDN_HEADS))).astype(BF16)
    proj, aux = _norm_proj(xt, norm_mix[0], w_main, w_aux)
    cos_r, sin_r = _rope_tables(seq, RET_DK // 2)
    log_gamma = jnp.log1p(-jnp.exp2(-5.0 - jnp.arange(RET_HEADS, dtype=F32)))
    log_gamma = jnp.broadcast_to(log_gamma[:, None, None], (RET_HEADS, 1, LANES))
    o_ret = _retention(proj, cos_r, sin_r, log_gamma, ret_norm[0].astype(F32), batch, seq)
    lane_pad = (GDN_HEADS, LANES - 2 * GDN_HEADS)
    alog_row = jnp.pad(gdn_a_log[0].astype(F32), lane_pad).reshape(1, LANES)
    dtb_row = jnp.pad(gdn_dt_bias[0].astype(F32), lane_pad).reshape(1, LANES)
    o_gdn = _gdn(proj, aux, gdn_conv[0].astype(F32), alog_row, dtb_row, gdn_norm[0].astype(F32), batch, seq)
    w_out = even_w_out[0].astype(BF16)
    split = RET_HEADS * RET_DV
    x1, h2, route = _out_proj([o_ret, o_gdn], [w_out[:split], w_out[split:]], xt, norm_ffn[0],
                              _router_weights(router_group[0], router_expert[0]))
    xt = _moe(x1, h2, route, expert_w_gate[0], expert_w_up[0], expert_w_down[0])

    proj = _norm_proj(xt, norm_mix[1], odd_w_in[0].astype(BF16), tn=ODD_IN // 2)
    cos_s, sin_s = _rope_tables(seq, SWA_HEAD_DIM // 2)
    o_swa = _swa(proj, cos_s, sin_s, q_norm[0], k_norm[0], attn_sinks[0], batch, seq)
    x1, h2, route = _out_proj([o_swa], [odd_w_out[0].astype(BF16)], xt, norm_ffn[1],
                              _router_weights(router_group[1], router_expert[1]))
    xt = _moe(x1, h2, route, expert_w_gate[1], expert_w_up[1], expert_w_down[1])
    return xt.reshape(batch, seq, d)
```

```python
import functools

import jax
import jax.numpy as jnp
from jax import lax
from jax.experimental import pallas as pl
from jax.experimental.pallas import tpu as pltpu

F32 = jnp.float32
BF16 = jnp.bfloat16

D_MODEL = 2048
RET_HEADS = 8
RET_DK = 128
RET_DV = 256
GDN_HEADS = 8
GDN_DK = 128
GDN_DV = 256
CONV_WIDTH = 4
SWA_Q_HEADS = 32
SWA_KV_HEADS = 4
SWA_HEAD_DIM = 64
WINDOW = 128
ROPE_THETA = 10000.0
N_GROUPS = 4
EXPERTS_PER_GROUP = 8
N_EXPERTS = N_GROUPS * EXPERTS_PER_GROUP
D_FF_EXPERT = 768
EPS = 1e-6
LOG2E = 1.4426950408889634

LANES = 128
SUBLANES = 8
VMEM_LIMIT = 56 * 1024 * 1024

EVEN_MAIN = 2 * RET_HEADS * RET_DK + 2 * RET_HEADS * RET_DV + 2 * GDN_HEADS * GDN_DK + 2 * GDN_HEADS * GDN_DV
ODD_IN = (SWA_Q_HEADS + 2 * SWA_KV_HEADS) * SWA_HEAD_DIM

RET_TILE = 256
RET_HEADS_PER_STEP = 4
GDN_TILE = 256
GDN_CHUNK = 128
GDN_HEADS_PER_STEP = 2
MOE_BM = 256
ROW_TILES = D_MODEL // LANES


def _cparams(sem):
    return pltpu.CompilerParams(dimension_semantics=sem, vmem_limit_bytes=VMEM_LIMIT)


def _sigmoid(x):
    return 1.0 / (1.0 + jnp.exp(-x))


def _nt_dot(a, b):
    return lax.dot_general(a, b, (((1,), (1,)), ((), ())), preferred_element_type=F32)


def _tn_dot(a, b):
    return lax.dot_general(a, b, (((0,), (0,)), ((), ())), preferred_element_type=F32)


def _rope_table_kernel(cos_ref, sin_ref, *, half):
    rows = cos_ref.shape[0]
    r0 = pl.program_id(0) * rows
    pos = (lax.broadcasted_iota(jnp.int32, (rows, LANES), 0) + r0).astype(F32)
    lane = lax.broadcasted_iota(jnp.int32, (rows, LANES), 1)
    fi = (lane % half).astype(F32)
    inv_freq = jnp.exp(-(fi / half) * jnp.log(ROPE_THETA))
    ang = pos * inv_freq
    first = (lane % (2 * half)) < half
    cos_ref[...] = jnp.cos(ang)
    sin_ref[...] = jnp.where(first, -jnp.sin(ang), jnp.sin(ang))


def _rope_tables(seq, half):
    rows = min(seq, 1024)
    return pl.pallas_call(
        functools.partial(_rope_table_kernel, half=half),
        grid=(seq // rows,),
        out_specs=[pl.BlockSpec((rows, LANES), lambda i: (i, 0))] * 2,
        out_shape=[jax.ShapeDtypeStruct((seq, LANES), F32)] * 2,
        compiler_params=_cparams(("arbitrary",)),
        name="rope_tables",
    )()


def _norm_proj_kernel(x_ref, g_ref, w_ref, *rest, with_aux):
    if with_aux:
        waux_ref, o_ref, aux_ref, h_scr = rest
    else:
        o_ref, h_scr = rest
    j = pl.program_id(1)

    @pl.when(j == 0)
    def _():
        x = x_ref[...]
        ms = jnp.mean(x * x, axis=-1, keepdims=True)
        h = x * lax.rsqrt(ms + EPS) * g_ref[...]
        h_scr[...] = h.astype(BF16)
        if with_aux:
            aux_ref[...] = jnp.dot(h_scr[...], waux_ref[...], preferred_element_type=F32)

    o_ref[...] = jnp.dot(h_scr[...], w_ref[...], preferred_element_type=F32).astype(o_ref.dtype)


def _norm_proj(x, gain, w, w_aux=None, *, tm=1024, tn=1024):
    n, d = x.shape
    n_out = w.shape[1]
    tm = min(tm, n)
    tn = min(tn, n_out)
    with_aux = w_aux is not None
    in_specs = [
        pl.BlockSpec((tm, d), lambda i, j: (i, 0)),
        pl.BlockSpec((1, d), lambda i, j: (0, 0)),
        pl.BlockSpec((d, tn), lambda i, j: (0, j)),
    ]
    out_specs = [pl.BlockSpec((tm, tn), lambda i, j: (i, j))]
    out_shape = [jax.ShapeDtypeStruct((n, n_out), BF16)]
    args = [x, gain.reshape(1, d), w]
    if with_aux:
        in_specs.append(pl.BlockSpec((d, LANES), lambda i, j: (0, 0)))
        out_specs.append(pl.BlockSpec((tm, LANES), lambda i, j: (i, 0)))
        out_shape.append(jax.ShapeDtypeStruct((n, LANES), F32))
        args.append(w_aux)
    res = pl.pallas_call(
        functools.partial(_norm_proj_kernel, with_aux=with_aux),
        grid=(n // tm, n_out // tn),
        in_specs=in_specs,
        out_specs=out_specs,
        out_shape=out_shape,
        scratch_shapes=[pltpu.VMEM((tm, d), BF16)],
        compiler_params=_cparams(("parallel", "arbitrary")),
        name="norm_proj",
    )(*args)
    return res if with_aux else res[0]


def _rope128(x, cos, sin):
    return x * cos + pltpu.roll(x, RET_DK // 2, 1) * sin


def _retention_kernel(q_ref, k_ref, v_ref, g_ref, cos_ref, sin_ref, lg_ref, gn_ref, o_ref,
                      state, dmat, qdec, kdec):
    t = pl.program_id(2)
    tile = q_ref.shape[0]
    hb = RET_HEADS_PER_STEP
    dot = functools.partial(jnp.dot, preferred_element_type=F32)
    lgs = [lg_ref[hh][:, :1] for hh in range(hb)]

    @pl.when(t == 0)
    def _():
        state[...] = jnp.zeros_like(state)
        ri = lax.broadcasted_iota(jnp.int32, (tile, tile), 0)
        ci = lax.broadcasted_iota(jnp.int32, (tile, tile), 1)
        causal = ri >= ci
        rel = jnp.where(causal, ri - ci, 0).astype(F32)
        pos = lax.broadcasted_iota(jnp.int32, (tile, RET_DK), 0).astype(F32)
        for hh, lg in enumerate(lgs):
            dmat[hh] = jnp.where(causal, jnp.exp(lg * rel), 0.0)
            qdec[hh] = jnp.exp(lg * (pos + 1.0))
            kdec[hh] = jnp.exp(lg * (tile - 1.0 - pos))

    cos = cos_ref[...]
    sin = sin_ref[...]
    heads = range(hb)
    qs = [_rope128(q_ref[:, hh * RET_DK:(hh + 1) * RET_DK].astype(F32), cos, sin) for hh in heads]
    ks = [_rope128(k_ref[:, hh * RET_DK:(hh + 1) * RET_DK].astype(F32), cos, sin) * (RET_DK ** -0.5)
          for hh in heads]
    vs = [v_ref[:, hh * RET_DV:(hh + 1) * RET_DV] for hh in heads]
    ss = [_nt_dot(qs[hh].astype(BF16), ks[hh].astype(BF16)) for hh in heads]
    sts = [state[hh] for hh in heads]
    cross = [dot((qs[hh] * qdec[hh]).astype(BF16), sts[hh].astype(BF16)) for hh in heads]
    upd = [_tn_dot((ks[hh] * kdec[hh]).astype(BF16), vs[hh]) for hh in heads]
    for hh in heads:
        state[hh] = sts[hh] * jnp.exp(lgs[hh] * float(tile)) + upd[hh]
    for hh in heads:
        o = dot((ss[hh] * dmat[hh]).astype(BF16), vs[hh]) + cross[hh]
        ms = jnp.mean(o * o, axis=-1, keepdims=True)
        y = o * lax.rsqrt(ms + EPS) * gn_ref[...]
        gate = g_ref[:, hh * RET_DV:(hh + 1) * RET_DV].astype(F32)
        o_ref[:, hh * RET_DV:(hh + 1) * RET_DV] = (y * (gate * _sigmoid(gate))).astype(o_ref.dtype)


def _retention(proj, cos, sin, log_gamma, ret_norm, batch, seq):
    tile = min(RET_TILE, seq)
    nt = seq // tile
    hb = RET_HEADS_PER_STEP
    dk, dv = hb * RET_DK, hb * RET_DV
    groups = RET_HEADS // hb
    qb = 0
    kb = groups
    vb = (2 * RET_HEADS * RET_DK) // dv
    gb = vb + groups
    row = lambda b, h, t: b * nt + t
    return pl.pallas_call(
        _retention_kernel,
        grid=(batch, groups, nt),
        in_specs=[
            pl.BlockSpec((tile, dk), lambda b, h, t: (row(b, h, t), qb + h)),
            pl.BlockSpec((tile, dk), lambda b, h, t: (row(b, h, t), kb + h)),
            pl.BlockSpec((tile, dv), lambda b, h, t: (row(b, h, t), vb + h)),
            pl.BlockSpec((tile, dv), lambda b, h, t: (row(b, h, t), gb + h)),
            pl.BlockSpec((tile, LANES), lambda b, h, t: (t, 0)),
            pl.BlockSpec((tile, LANES), lambda b, h, t: (t, 0)),
            pl.BlockSpec((hb, 1, LANES), lambda b, h, t: (h, 0, 0)),
            pl.BlockSpec((1, RET_DV), lambda b, h, t: (0, 0)),
        ],
        out_specs=pl.BlockSpec((tile, dv), lambda b, h, t: (row(b, h, t), h)),
        out_shape=jax.ShapeDtypeStruct((batch * seq, RET_HEADS * RET_DV), BF16),
        scratch_shapes=[
            pltpu.VMEM((hb, RET_DK, RET_DV), F32),
            pltpu.VMEM((hb, tile, tile), F32),
            pltpu.VMEM((hb, tile, RET_DK), F32),
            pltpu.VMEM((hb, tile, RET_DK), F32),
        ],
        compiler_params=_cparams(("parallel", "parallel", "arbitrary")),
        name="retention",
    )(proj, proj, proj, proj, cos, sin, log_gamma, ret_norm.reshape(1, RET_DV))


def _conv_silu(xbuf, x_ref, w_ref):
    tile = x_ref.shape[0]
    xbuf[SUBLANES:SUBLANES + tile, :] = x_ref[...].astype(F32)
    w = w_ref[...]
    base = SUBLANES - (CONV_WIDTH - 1)
    acc = xbuf[base:base + tile, :] * w[0:1, :]
    for j in range(1, CONV_WIDTH):
        acc = acc + xbuf[base + j:base + j + tile, :] * w[j:j + 1, :]
    xbuf[0:SUBLANES, :] = xbuf[tile:tile + SUBLANES, :]
    return acc * _sigmoid(acc)


def _l2norm(x):
    return x * lax.rsqrt(jnp.sum(x * x, axis=-1, keepdims=True) + EPS)


def _softplus(x):
    return jnp.maximum(x, 0.0) + jnp.log1p(jnp.exp(-jnp.abs(x)))


def _unit_lower_inverses(lows):
    c = lows[0].shape[0]
    dot = functools.partial(jnp.dot, preferred_element_type=F32)
    eye = (lax.broadcasted_iota(jnp.int32, (c, c), 0) == lax.broadcasted_iota(jnp.int32, (c, c), 1)).astype(F32)
    invs = [eye - low for low in lows]
    powers = [_split_bf16(low) for low in lows]
    span = 2
    while span < c:
        powers = [_split_bf16(_dot_split(hi, lo, hi, lo)) for hi, lo in powers]
        invs = [inv + _dot_split(*_split_bf16(inv), hi, lo) for inv, (hi, lo) in zip(invs, powers)]
        span *= 2
    return invs


def _split_bf16(a):
    hi = a.astype(BF16)
    return hi, (a - hi.astype(F32)).astype(BF16)


def _dot_split(a_hi, a_lo, b_hi, b_lo):
    dot = functools.partial(jnp.dot, preferred_element_type=F32)
    return dot(a_hi, b_hi) + dot(a_hi, b_lo) + dot(a_lo, b_hi)


def _chunk_cumsum(tril_b, g):
    dot = functools.partial(jnp.dot, preferred_element_type=F32)
    g1 = g.astype(BF16)
    r1 = g - g1.astype(F32)
    g2 = r1.astype(BF16)
    g3 = (r1 - g2.astype(F32)).astype(BF16)
    return dot(tril_b, g1) + dot(tril_b, g2) + dot(tril_b, g3)


def _gdn_kernel(xq_ref, xk_ref, xv_ref, z_ref, aux_ref, wq_ref, wk_ref, wv_ref, alog_ref, dtb_ref,
                gn_ref, o_ref, state, qbuf, kbuf, vbuf):
    hp = pl.program_id(1)
    t = pl.program_id(2)
    tile = xq_ref.shape[0]
    c = GDN_CHUNK
    hb = GDN_HEADS_PER_STEP

    @pl.when(t == 0)
    def _():
        state[...] = jnp.zeros_like(state)
        qbuf[0:SUBLANES, :] = jnp.zeros((SUBLANES, qbuf.shape[1]), F32)
        kbuf[0:SUBLANES, :] = jnp.zeros((SUBLANES, kbuf.shape[1]), F32)
        vbuf[0:SUBLANES, :] = jnp.zeros((SUBLANES, vbuf.shape[1]), F32)

    q_all = _conv_silu(qbuf, xq_ref, wq_ref)
    k_all = _conv_silu(kbuf, xk_ref, wk_ref)
    v_all = _conv_silu(vbuf, xv_ref, wv_ref)

    aux = aux_ref[...]
    lane = lax.broadcasted_iota(jnp.int32, (tile, LANES), 1)
    beta_all = _sigmoid(aux)
    g_all = -jnp.exp(alog_ref[...]) * _softplus(aux + dtb_ref[...])

    ri = lax.broadcasted_iota(jnp.int32, (c, c), 0)
    ci = lax.broadcasted_iota(jnp.int32, (c, c), 1)
    causal = ri >= ci
    strict = ri > ci
    rt = lax.broadcasted_iota(jnp.int32, (tile, tile), 0)
    ct = lax.broadcasted_iota(jnp.int32, (tile, tile), 1)
    same_chunk = lax.shift_right_logical(rt, c.bit_length() - 1) == lax.shift_right_logical(ct, c.bit_length() - 1)
    gcum_all = _chunk_cumsum(((rt >= ct) & same_chunk).astype(BF16), g_all)

    dot = functools.partial(jnp.dot, preferred_element_type=F32)
    nchunk = tile // c
    pairs = []
    for hh in range(hb):
        h = hp * hb + hh
        q = _l2norm(q_all[:, hh * GDN_DK:(hh + 1) * GDN_DK]) * (GDN_DK ** -0.5)
        k = _l2norm(k_all[:, hh * GDN_DK:(hh + 1) * GDN_DK])
        v = v_all[:, hh * GDN_DV:(hh + 1) * GDN_DV]
        beta = jnp.sum(jnp.where(lane == h, beta_all, 0.0), axis=1, keepdims=True)
        gc = jnp.sum(jnp.where(lane == h + GDN_HEADS, gcum_all, 0.0), axis=1, keepdims=True)
        for i in range(nchunk):
            sl = slice(i * c, (i + 1) * c)
            qc, kc, vc, bc = q[sl], k[sl], v[sl], beta[sl]
            gcum = jnp.broadcast_to(gc[sl], (c, c))
            rel = gcum - gcum.T
            decay = jnp.where(causal, jnp.exp(jnp.where(causal, rel, 0.0)), 0.0)
            eg = jnp.exp(gcum)
            kb = kc * bc
            kcb = kc.astype(BF16)
            g_last = gcum[c - 1:c, :]
            pairs.append(dict(
                hh=hh, i=i,
                low=jnp.where(strict, _nt_dot(kb.astype(BF16), kcb) * decay, 0.0),
                attn=jnp.where(causal, _nt_dot(qc.astype(BF16), kcb) * decay, 0.0).astype(BF16),
                vb=(vc * bc).astype(BF16),
                kbe=(kb * eg).astype(BF16),
                qg=(qc * eg).astype(BF16),
                k_tail_t=(kc * jnp.exp(g_last - gcum)).T.astype(BF16),
                sdec=jnp.exp(g_last[:, :1]),
            ))
    pairs.sort(key=lambda p: (p["i"], p["hh"]))
    invs = _unit_lower_inverses([p["low"] for p in pairs])
    for p, inv in zip(pairs, invs):
        inv_b = inv.astype(BF16)
        p["u"] = dot(inv_b, p["vb"])
        p["w"] = dot(inv_b, p["kbe"]).astype(BF16)
    for p in pairs:
        hh, sl = p["hh"], slice(p["i"] * c, (p["i"] + 1) * c)
        st = state[hh]
        stb = st.astype(BF16)
        v_new = (p["u"] - dot(p["w"], stb)).astype(BF16)
        state[hh] = st * p["sdec"] + dot(p["k_tail_t"], v_new)
        o = dot(p["qg"], stb) + dot(p["attn"], v_new)
        ms = jnp.mean(o * o, axis=-1, keepdims=True)
        y = o * lax.rsqrt(ms + EPS) * gn_ref[...]
        z = z_ref[sl, hh * GDN_DV:(hh + 1) * GDN_DV].astype(F32)
        o_ref[sl, hh * GDN_DV:(hh + 1) * GDN_DV] = (y * (z * _sigmoid(z))).astype(o_ref.dtype)


def _gdn(proj, aux, conv_w, alog_row, dtb_row, gdn_norm, batch, seq):
    tile = min(GDN_TILE, seq)
    nt = seq // tile
    hb = GDN_HEADS_PER_STEP
    dk, dv = hb * GDN_DK, hb * GDN_DV
    base = 2 * RET_HEADS * RET_DK + 2 * RET_HEADS * RET_DV
    qb = base // dk
    kb = qb + GDN_HEADS // hb
    vb = (base + 2 * GDN_HEADS * GDN_DK) // dv
    zb = vb + GDN_HEADS // hb
    cvb = (2 * GDN_HEADS * GDN_DK) // dv
    row = lambda b, h, t: b * nt + t
    return pl.pallas_call(
        _gdn_kernel,
        grid=(batch, GDN_HEADS // hb, nt),
        in_specs=[
            pl.BlockSpec((tile, dk), lambda b, h, t: (row(b, h, t), qb + h)),
            pl.BlockSpec((tile, dk), lambda b, h, t: (row(b, h, t), kb + h)),
            pl.BlockSpec((tile, dv), lambda b, h, t: (row(b, h, t), vb + h)),
            pl.BlockSpec((tile, dv), lambda b, h, t: (row(b, h, t), zb + h)),
            pl.BlockSpec((tile, LANES), lambda b, h, t: (row(b, h, t), 0)),
            pl.BlockSpec((CONV_WIDTH, dk), lambda b, h, t: (0, h)),
            pl.BlockSpec((CONV_WIDTH, dk), lambda b, h, t: (0, GDN_HEADS // hb + h)),
            pl.BlockSpec((CONV_WIDTH, dv), lambda b, h, t: (0, cvb + h)),
            pl.BlockSpec((1, LANES), lambda b, h, t: (0, 0)),
            pl.BlockSpec((1, LANES), lambda b, h, t: (0, 0)),
            pl.BlockSpec((1, GDN_DV), lambda b, h, t: (0, 0)),
        ],
        out_specs=pl.BlockSpec((tile, dv), lambda b, h, t: (row(b, h, t), h)),
        out_shape=jax.ShapeDtypeStruct((batch * seq, GDN_HEADS * GDN_DV), BF16),
        scratch_shapes=[
            pltpu.VMEM((hb, GDN_DK, GDN_DV), F32),
            pltpu.VMEM((tile + SUBLANES, dk), F32),
            pltpu.VMEM((tile + SUBLANES, dk), F32),
            pltpu.VMEM((tile + SUBLANES, dv), F32),
        ],
        compiler_params=_cparams(("parallel", "parallel", "arbitrary")),
        name="gated_deltanet",
    )(proj, proj, proj, proj, aux, conv_w, conv_w, conv_w, alog_row, dtb_row, gdn_norm.reshape(1, GDN_DV))


def _rope64(x, cos, sin):
    half = SWA_HEAD_DIM // 2
    lane = lax.broadcasted_iota(jnp.int32, x.shape, x.ndim - 1)
    first = (lane % SWA_HEAD_DIM) < half
    rot = jnp.where(first, pltpu.roll(x, LANES - half, x.ndim - 1), pltpu.roll(x, half, x.ndim - 1))
    return x * cos + rot * sin


def _head_rmsnorm(x, gain, seg_ones):
    sq_hi, sq_lo = _split_bf16(x * x)
    ss = jnp.dot(sq_hi, seg_ones, preferred_element_type=F32) + jnp.dot(sq_lo, seg_ones, preferred_element_type=F32)
    return x * lax.rsqrt(ss * (1.0 / SWA_HEAD_DIM) + EPS) * gain


def _swa_kernel(sink_ref, q_ref, k_ref, v_ref, cos_ref, sin_ref, qn_ref, kn_ref, seg_ref, o_ref,
                kprev, vprev):
    i = pl.program_id(1)
    w = WINDOW
    pairs = SWA_KV_HEADS // 2
    grp = SWA_Q_HEADS // SWA_KV_HEADS
    qchunks = grp * SWA_HEAD_DIM // LANES

    @pl.when(i == 0)
    def _():
        kprev[...] = jnp.zeros_like(kprev)
        vprev[...] = jnp.zeros_like(vprev)

    cos = cos_ref[...]
    sin = sin_ref[...]
    seg = seg_ref[...]
    lane = lax.broadcasted_iota(jnp.int32, (w, LANES), 1)
    lo_mask = lane < SWA_HEAD_DIM

    k_lo, k_hi, v_lo, v_hi = [], [], [], []
    for c in range(pairs):
        kc = k_ref[:, c * LANES:(c + 1) * LANES].astype(F32)
        kc = _rope64(_head_rmsnorm(kc, kn_ref[...], seg), cos, sin)
        vc = v_ref[:, c * LANES:(c + 1) * LANES].astype(F32)
        for src, lo_list, hi_list in ((kc, k_lo, k_hi), (vc, v_lo, v_hi)):
            a_lo = jnp.where(lo_mask, src, 0.0)
            b_hi = jnp.where(lo_mask, 0.0, src)
            lo_list += [a_lo, pltpu.roll(b_hi, SWA_HEAD_DIM, 1)]
            hi_list += [pltpu.roll(a_lo, SWA_HEAD_DIM, 1), b_hi]

    qpos = lax.broadcasted_iota(jnp.int32, (w, 2 * w), 0) + w
    kpos = lax.broadcasted_iota(jnp.int32, (w, 2 * w), 1)
    rel = qpos - kpos
    first_key = jnp.where(i > 0, 0, w)
    valid = (rel >= 0) & (rel < w) & (kpos >= first_key)
    scale = SWA_HEAD_DIM ** -0.5 * LOG2E
    cos_q = jnp.concatenate([cos] * qchunks, axis=0)
    sin_q = jnp.concatenate([sin] * qchunks, axis=0)

    for kvh in range(SWA_KV_HEADS):
        q0 = kvh * grp * SWA_HEAD_DIM
        q2 = jnp.concatenate(
            [q_ref[:, q0 + c * LANES:q0 + (c + 1) * LANES].astype(F32) for c in range(qchunks)], axis=0)
        q2 = _head_rmsnorm(q2, qn_ref[...], seg)
        q2 = (_rope64(q2, cos_q, sin_q) * scale).astype(BF16)
        kl = jnp.concatenate([kprev[0, kvh], k_lo[kvh].astype(BF16)], axis=0)
        kh = jnp.concatenate([kprev[1, kvh], k_hi[kvh].astype(BF16)], axis=0)
        vl = jnp.concatenate([vprev[0, kvh], v_lo[kvh].astype(BF16)], axis=0)
        vh = jnp.concatenate([vprev[1, kvh], v_hi[kvh].astype(BF16)], axis=0)
        acc = None
        for par, kk, vv in ((0, kl, vl), (1, kh, vh)):
            sc = _nt_dot(q2, kk)
            probs, denoms = [], []
            for c in range(qchunks):
                sink = sink_ref[kvh * grp + 2 * c + par] * LOG2E
                s = jnp.where(valid, sc[c * w:(c + 1) * w, :], -jnp.inf)
                mx = jnp.maximum(jnp.max(s, axis=-1, keepdims=True), sink)
                p = jnp.exp2(s - mx)
                denoms.append(jnp.sum(p, axis=-1, keepdims=True) + jnp.exp2(sink - mx))
                probs.append(p.astype(BF16))
            pv = jnp.dot(jnp.concatenate(probs, axis=0), vv, preferred_element_type=F32)
            pv = [pv[c * w:(c + 1) * w, :] * (1.0 / denoms[c]) for c in range(qchunks)]
            acc = pv if acc is None else [a + b for a, b in zip(acc, pv)]
        for c in range(qchunks):
            o_ref[:, q0 + c * LANES:q0 + (c + 1) * LANES] = acc[c].astype(o_ref.dtype)

    for kvh in range(SWA_KV_HEADS):
        kprev[0, kvh] = k_lo[kvh].astype(BF16)
        kprev[1, kvh] = k_hi[kvh].astype(BF16)
        vprev[0, kvh] = v_lo[kvh].astype(BF16)
        vprev[1, kvh] = v_hi[kvh].astype(BF16)


def _swa(proj, cos, sin, q_norm, k_norm, sinks, batch, seq):
    w = WINDOW
    nb = seq // w
    qw = SWA_Q_HEADS * SWA_HEAD_DIM
    kvw = SWA_KV_HEADS * SWA_HEAD_DIM
    kblk = qw // kvw
    reps = LANES // SWA_HEAD_DIM
    seg = (jnp.arange(LANES)[:, None] // SWA_HEAD_DIM == jnp.arange(LANES)[None, :] // SWA_HEAD_DIM).astype(BF16)
    return pl.pallas_call(
        _swa_kernel,
        grid=(batch, nb),
        in_specs=[
            pl.BlockSpec(memory_space=pltpu.SMEM),
            pl.BlockSpec((w, qw), lambda b, i: (b * nb + i, 0)),
            pl.BlockSpec((w, kvw), lambda b, i: (b * nb + i, kblk)),
            pl.BlockSpec((w, kvw), lambda b, i: (b * nb + i, kblk + 1)),
            pl.BlockSpec((w, LANES), lambda b, i: (i, 0)),
            pl.BlockSpec((w, LANES), lambda b, i: (i, 0)),
            pl.BlockSpec((1, LANES), lambda b, i: (0, 0)),
            pl.BlockSpec((1, LANES), lambda b, i: (0, 0)),
            pl.BlockSpec((LANES, LANES), lambda b, i: (0, 0)),
        ],
        out_specs=pl.BlockSpec((w, qw), lambda b, i: (b * nb + i, 0)),
        out_shape=jax.ShapeDtypeStruct((batch * seq, qw), BF16),
        scratch_shapes=[
            pltpu.VMEM((2, SWA_KV_HEADS, w, LANES), BF16),
            pltpu.VMEM((2, SWA_KV_HEADS, w, LANES), BF16),
        ],
        compiler_params=_cparams(("parallel", "arbitrary")),
        name="swa",
    )(sinks.astype(F32), proj, proj, proj, cos, sin,
      jnp.tile(q_norm.astype(F32), reps).reshape(1, LANES),
      jnp.tile(k_norm.astype(F32), reps).reshape(1, LANES), seg)


def _route(logits):
    tm = logits.shape[0]
    lane = lax.broadcasted_iota(jnp.int32, (tm, LANES), 1)
    lane_f = lane.astype(F32)
    big = float(LANES)
    neg = -jnp.inf
    gl = jnp.where(lane < N_GROUPS, logits, neg)
    gmax = jnp.max(gl, axis=1, keepdims=True)
    gidx = jnp.min(jnp.where(gl == gmax, lane_f, big), axis=1, keepdims=True)
    g_p = 1.0 / jnp.sum(jnp.exp(gl - gmax), axis=1, keepdims=True)
    egroup = lax.shift_right_arithmetic(lane - N_GROUPS, 3).astype(F32)
    in_group = (lane >= N_GROUPS) & (lane < N_GROUPS + N_EXPERTS) & (egroup == gidx)
    el = jnp.where(in_group, logits, neg)
    emax = jnp.max(el, axis=1, keepdims=True)
    ee = jnp.exp(el - emax)
    prob = ee / jnp.sum(ee, axis=1, keepdims=True)
    pm = jnp.where(in_group, prob, -1.0)
    p1 = jnp.max(pm, axis=1, keepdims=True)
    i1 = jnp.min(jnp.where(pm == p1, lane_f, big), axis=1, keepdims=True)
    pm2 = jnp.where(lane_f == i1, -1.0, pm)
    p2 = jnp.max(pm2, axis=1, keepdims=True)
    i2 = jnp.min(jnp.where(pm2 == p2, lane_f, big), axis=1, keepdims=True)
    den = p1 + p2
    out = jnp.where(lane == 0, i1 - N_GROUPS, 0.0)
    out = jnp.where(lane == 1, i2 - N_GROUPS, out)
    out = jnp.where(lane == 2, g_p * p1 / den, out)
    out = jnp.where(lane == 3, g_p * p2 / den, out)
    return out


def _out_proj_kernel(*refs, n_act):
    acts = refs[:n_act]
    ws = refs[n_act:2 * n_act]
    x_ref, g_ref, wr_ref, x1_ref, h_ref, route_ref = refs[2 * n_act:]
    y = x_ref[...]
    for a, w in zip(acts, ws):
        y = y + jnp.dot(a[...], w[...], preferred_element_type=F32)
    x1_ref[...] = y
    ms = jnp.mean(y * y, axis=-1, keepdims=True)
    h = y * lax.rsqrt(ms + EPS) * g_ref[...]
    tm = y.shape[0]
    for s in range(ROW_TILES):
        h_ref[pl.ds(s, tm, stride=ROW_TILES), :] = h[:, s * LANES:(s + 1) * LANES]
    h_hi, h_lo = _split_bf16(h)
    route_ref[...] = _route(_dot_split(h_hi, h_lo, wr_ref[0], wr_ref[1]))


def _out_proj(acts, ws, x, gain, w_router, *, tm=256):
    n, d = x.shape
    tm = min(tm, n)
    n_act = len(acts)
    const = lambda i: (0, 0)
    in_specs = [pl.BlockSpec((tm, a.shape[1]), lambda i: (i, 0)) for a in acts]
    in_specs += [pl.BlockSpec(w.shape, const, pipeline_mode=pl.Buffered(1)) for w in ws]
    in_specs += [
        pl.BlockSpec((tm, d), lambda i: (i, 0)),
        pl.BlockSpec((1, d), const),
        pl.BlockSpec((2, d, LANES), lambda i: (0, 0, 0)),
    ]
    return pl.pallas_call(
        functools.partial(_out_proj_kernel, n_act=n_act),
        grid=(n // tm,),
        in_specs=in_specs,
        out_specs=[
            pl.BlockSpec((tm, d), lambda i: (i, 0)),
            pl.BlockSpec((tm * ROW_TILES, LANES), lambda i: (i, 0)),
            pl.BlockSpec((tm, LANES), lambda i: (i, 0)),
        ],
        out_shape=[
            jax.ShapeDtypeStruct((n, d), F32),
            jax.ShapeDtypeStruct((n * ROW_TILES, LANES), F32),
            jax.ShapeDtypeStruct((n, LANES), F32),
        ],
        compiler_params=_cparams(("parallel",)),
        name="out_proj_router",
    )(*acts, *ws, x, gain.reshape(1, d), w_router)


def _row_gather_start(idx_ref, base, count, src_hbm, dst, sem, *, both_queues):
    group = 8

    def body(gi, carry):
        for j in range(group):
            r = gi * group + j
            tok = idx_ref[base + r]
            pltpu.make_async_copy(src_hbm.at[pl.ds(pl.multiple_of(tok * ROW_TILES, ROW_TILES), ROW_TILES), :],
                                  dst.at[pl.ds(pl.multiple_of(r * ROW_TILES, ROW_TILES), ROW_TILES), :],
                                  sem).start(priority=j % 2 if both_queues else 0)
        return carry
    lax.fori_loop(0, count // group, body, 0)


def _row_gather_wait(dst, sem):
    pltpu.make_async_copy(dst, dst, sem).wait()


def _untile_rows(buf, rows):
    return [buf[pl.ds(s, rows, stride=ROW_TILES), :] for s in range(ROW_TILES)]


def _expert_kernel(be_ref, nu_ref, nxt_ref, tok_ref, h_hbm, wg_hbm, wu_hbm, wd_hbm, o_ref,
                   xbuf, stage_g, stage_u, stage_d, wg_b, wu_b, wd_b, gsem, wsem, *, layer):
    i = pl.program_id(0)
    bm = MOE_BM
    nu = nu_ref[0]
    slot = i % 2
    e = be_ref[i]
    weight_queue = 1

    def weight_copies(expert):
        return (pltpu.make_async_copy(wg_hbm.at[layer, expert], stage_g, wsem.at[0]),
                pltpu.make_async_copy(wu_hbm.at[layer, expert], stage_u, wsem.at[1]),
                pltpu.make_async_copy(wd_hbm.at[layer, expert], stage_d, wsem.at[2]))

    @pl.when(i == 0)
    def _():
        _row_gather_start(tok_ref, 0, bm, h_hbm, xbuf.at[0], gsem.at[0], both_queues=False)
        for cp in weight_copies(e):
            cp.start(priority=weight_queue)

    @pl.when(i < nu)
    def _():
        @pl.when(i + 1 < nu)
        def _():
            _row_gather_start(tok_ref, (i + 1) * bm, bm, h_hbm, xbuf.at[1 - slot], gsem.at[1 - slot],
                              both_queues=False)

        first_of_run = jnp.logical_or(i == 0, e != be_ref[jnp.maximum(i - 1, 0)])

        @pl.when(first_of_run)
        def _():
            for cp, stage, dst in zip(weight_copies(e), (stage_g, stage_u, stage_d), (wg_b, wu_b, wd_b)):
                cp.wait()
                dst[...] = stage[...].astype(BF16)
            nxt = nxt_ref[e]

            @pl.when(nxt >= 0)
            def _():
                for cp in weight_copies(nxt):
                    cp.start(priority=weight_queue)

        _row_gather_wait(xbuf.at[slot], gsem.at[slot])
        x = jnp.concatenate([c.astype(BF16) for c in _untile_rows(xbuf.at[slot], bm)], axis=1)
        g = jnp.dot(x, wg_b[...], preferred_element_type=F32)
        u = jnp.dot(x, wu_b[...], preferred_element_type=F32)
        hid = (g * _sigmoid(g) * u).astype(BF16)
        y = jnp.dot(hid, wd_b[...], preferred_element_type=F32)
        for s in range(ROW_TILES):
            o_ref[pl.ds(s, bm, stride=ROW_TILES), :] = y[:, s * LANES:(s + 1) * LANES]

    @pl.when(i >= nu)
    def _():
        o_ref[...] = jnp.zeros_like(o_ref)


def _expert_mlp(h_tiled, tok_pad, block_expert, n_used, next_expert, w_gate, w_up, w_down, layer):
    n_pad = tok_pad.shape[0]
    bm = MOE_BM
    _, _, d, ff = w_gate.shape
    any_spec = pl.BlockSpec(memory_space=pl.ANY)
    grid_spec = pltpu.PrefetchScalarGridSpec(
        num_scalar_prefetch=4,
        grid=(n_pad // bm,),
        in_specs=[any_spec, any_spec, any_spec, any_spec],
        out_specs=pl.BlockSpec((bm * ROW_TILES, LANES), lambda i, *_: (i, 0)),
        scratch_shapes=[
            pltpu.VMEM((2, bm * ROW_TILES, LANES), F32),
            pltpu.VMEM((d, ff), F32), pltpu.VMEM((d, ff), F32), pltpu.VMEM((ff, d), F32),
            pltpu.VMEM((d, ff), BF16), pltpu.VMEM((d, ff), BF16), pltpu.VMEM((ff, d), BF16),
            pltpu.SemaphoreType.DMA((2,)),
            pltpu.SemaphoreType.DMA((3,)),
        ],
    )
    return pl.pallas_call(
        functools.partial(_expert_kernel, layer=layer),
        grid_spec=grid_spec,
        out_shape=jax.ShapeDtypeStruct((n_pad * ROW_TILES, LANES), F32),
        compiler_params=_cparams(("arbitrary",)),
        name="expert_mlp",
    )(block_expert, n_used, next_expert, tok_pad, h_tiled, w_gate, w_up, w_down)


def _combine_kernel(dest_ref, x1_ref, route_ref, y_hbm, o_ref, ybuf, sem):
    i = pl.program_id(0)
    nsteps = pl.num_programs(0)
    tm = x1_ref.shape[0]
    slot = i % 2

    def start(step, sl):
        for k in range(2):
            _row_gather_start(dest_ref, k * (nsteps * tm) + step * tm, tm, y_hbm, ybuf.at[sl, k], sem.at[sl],
                              both_queues=True)

    @pl.when(i == 0)
    def _():
        start(0, 0)

    @pl.when(i + 1 < nsteps)
    def _():
        start(i + 1, 1 - slot)

    _row_gather_wait(ybuf.at[slot], sem.at[slot])
    route = route_ref[...]
    g0 = route[:, 2:3]
    g1 = route[:, 3:4]
    for s in range(ROW_TILES):
        y0 = ybuf[slot, 0, pl.ds(s, tm, stride=ROW_TILES), :]
        y1 = ybuf[slot, 1, pl.ds(s, tm, stride=ROW_TILES), :]
        o_ref[:, s * LANES:(s + 1) * LANES] = x1_ref[:, s * LANES:(s + 1) * LANES] + g0 * y0 + g1 * y1


def _combine(x1, route, dest, y_tiled, *, tm=256):
    n, d = x1.shape
    tm = min(tm, n)
    grid_spec = pltpu.PrefetchScalarGridSpec(
        num_scalar_prefetch=1,
        grid=(n // tm,),
        in_specs=[
            pl.BlockSpec((tm, d), lambda i, *_: (i, 0)),
            pl.BlockSpec((tm, LANES), lambda i, *_: (i, 0)),
            pl.BlockSpec(memory_space=pl.ANY),
        ],
        out_specs=pl.BlockSpec((tm, d), lambda i, *_: (i, 0)),
        scratch_shapes=[
            pltpu.VMEM((2, 2, tm * ROW_TILES, LANES), F32),
            pltpu.SemaphoreType.DMA((2,)),
        ],
    )
    return pl.pallas_call(
        _combine_kernel,
        grid_spec=grid_spec,
        out_shape=jax.ShapeDtypeStruct((n, d), F32),
        compiler_params=_cparams(("arbitrary",)),
        name="moe_combine",
    )(dest, x1, route, y_tiled)


def _moe(x1, h_tiled, route, w_gate, w_up, w_down, layer):
    n, d = x1.shape
    bm = MOE_BM
    n_assign = 2 * n
    expert_id = route[:, 0:2].astype(jnp.int32).reshape(-1)
    onehot = (expert_id[:, None] == jnp.arange(N_EXPERTS, dtype=jnp.int32)[None, :]).astype(jnp.int32)
    csum = jnp.cumsum(onehot, axis=0)
    rank = jnp.sum(csum * onehot, axis=1) - 1
    counts = csum[-1]
    padded = (counts + bm - 1) // bm * bm
    pend = jnp.cumsum(padded)
    pstart = pend - padded
    dest = (jnp.sum(onehot * pstart[None, :], axis=1) + rank).astype(jnp.int32)
    n_pad = (n_assign + N_EXPERTS * (bm - 1) + bm - 1) // bm * bm
    n_blk = n_pad // bm
    token_id = jnp.arange(n_assign, dtype=jnp.int32) // 2
    tok_pad = (jnp.arange(n_pad, dtype=jnp.int32) % n).at[dest].set(token_id)
    block_start = jnp.arange(n_blk, dtype=jnp.int32) * bm
    block_expert = jnp.minimum(jnp.sum(pend[None, :] <= block_start[:, None], axis=1), N_EXPERTS - 1)
    n_used = (pend[-1] // bm).astype(jnp.int32).reshape(1)
    ids = jnp.arange(N_EXPERTS, dtype=jnp.int32)
    cand = jnp.where((counts[None, :] > 0) & (ids[None, :] > ids[:, None]), ids[None, :], N_EXPERTS)
    next_expert = jnp.min(cand, axis=1)
    next_expert = jnp.where(next_expert >= N_EXPERTS, -1, next_expert).astype(jnp.int32)
    y_tiled = _expert_mlp(h_tiled, tok_pad, block_expert.astype(jnp.int32), n_used, next_expert,
                          w_gate, w_up, w_down, layer)
    dest_by_choice = dest.reshape(n, 2).T.reshape(-1).astype(jnp.int32)
    return _combine(x1, route, dest_by_choice, y_tiled)


def _router_weights(w_group, w_expert):
    d = w_group.shape[0]
    pad = jnp.zeros((d, LANES - N_GROUPS - N_EXPERTS), F32)
    w = jnp.concatenate([w_group.astype(F32), w_expert.astype(F32), pad], axis=1)
    w_hi = w.astype(BF16)
    w_lo = (w - w_hi.astype(F32)).astype(BF16)
    return jnp.stack([w_hi, w_lo])


def kernel(x, norm_mix, norm_ffn, even_w_in, ret_norm, gdn_conv, gdn_a_log, gdn_dt_bias, gdn_norm,
           even_w_out, odd_w_in, q_norm, k_norm, attn_sinks, odd_w_out, router_group, router_expert,
           expert_w_gate, expert_w_up, expert_w_down):
    batch, seq, d = x.shape
    n = batch * seq
    xt = x.reshape(n, d)

    w_in = even_w_in[0]
    w_main = w_in[:, :EVEN_MAIN].astype(BF16)
    w_aux = jnp.pad(w_in[:, EVEN_MAIN:], ((0, 0), (0, LANES - 2 * GDN_HEADS))).astype(BF16)
    proj, aux = _norm_proj(xt, norm_mix[0], w_main, w_aux)
    cos_r, sin_r = _rope_tables(seq, RET_DK // 2)
    log_gamma = jnp.log1p(-jnp.exp2(-5.0 - jnp.arange(RET_HEADS, dtype=F32)))
    log_gamma = jnp.broadcast_to(log_gamma[:, None, None], (RET_HEADS, 1, LANES))
    o_ret = _retention(proj, cos_r, sin_r, log_gamma, ret_norm[0].astype(F32), batch, seq)
    lane_pad = (GDN_HEADS, LANES - 2 * GDN_HEADS)
    alog_row = jnp.pad(gdn_a_log[0].astype(F32), lane_pad).reshape(1, LANES)
    dtb_row = jnp.pad(gdn_dt_bias[0].astype(F32), lane_pad).reshape(1, LANES)
    o_gdn = _gdn(proj, aux, gdn_conv[0].astype(F32), alog_row, dtb_row, gdn_norm[0].astype(F32), batch, seq)
    w_out = even_w_out[0].astype(BF16)
    split = RET_HEADS * RET_DV
    x1, h2, route = _out_proj([o_ret, o_gdn], [w_out[:split], w_out[split:]], xt, norm_ffn[0],
                              _router_weights(router_group[0], router_expert[0]))
    xt = _moe(x1, h2, route, expert_w_gate, expert_w_up, expert_w_down, 0)

    proj = _norm_proj(xt, norm_mix[1], odd_w_in[0].astype(BF16), tn=ODD_IN // 2)
    cos_s, sin_s = _rope_tables(seq, SWA_HEAD_DIM // 2)
    o_swa = _swa(proj, cos_s, sin_s, q_norm[0], k_norm[0], attn_sinks[0], batch, seq)
    x1, h2, route = _out_proj([o_swa], [odd_w_out[0].astype(BF16)], xt, norm_ffn[1],
                              _router_weights(router_group[1], router_expert[1]))
    xt = _moe(x1, h2, route, expert_w_gate, expert_w_up, expert_w_down, 1)
    return xt.reshape(batch, seq, d)
```

```python
import functools

import jax
import jax.numpy as jnp
from jax import lax
from jax.experimental import pallas as pl
from jax.experimental.pallas import tpu as pltpu

F32 = jnp.float32
BF16 = jnp.bfloat16

D_MODEL = 2048
RET_HEADS = 8
RET_DK = 128
RET_DV = 256
GDN_HEADS = 8
GDN_DK = 128
GDN_DV = 256
CONV_WIDTH = 4
SWA_Q_HEADS = 32
SWA_KV_HEADS = 4
SWA_HEAD_DIM = 64
WINDOW = 128
ROPE_THETA = 10000.0
N_GROUPS = 4
EXPERTS_PER_GROUP = 8
N_EXPERTS = N_GROUPS * EXPERTS_PER_GROUP
D_FF_EXPERT = 768
EPS = 1e-6
LOG2E = 1.4426950408889634

LANES = 128
SUBLANES = 8
VMEM_LIMIT = 56 * 1024 * 1024

EVEN_MAIN = 2 * RET_HEADS * RET_DK + 2 * RET_HEADS * RET_DV + 2 * GDN_HEADS * GDN_DK + 2 * GDN_HEADS * GDN_DV
ODD_IN = (SWA_Q_HEADS + 2 * SWA_KV_HEADS) * SWA_HEAD_DIM

RET_TILE = 256
RET_HEADS_PER_STEP = 4
GDN_TILE = 256
GDN_CHUNK = 128
GDN_HEADS_PER_STEP = 8
MOE_BM = 256
ROW_TILES = D_MODEL // LANES


def _cparams(sem):
    return pltpu.CompilerParams(dimension_semantics=sem, vmem_limit_bytes=VMEM_LIMIT)


def _sigmoid(x):
    return 1.0 / (1.0 + jnp.exp(-x))


def _nt_dot(a, b):
    return lax.dot_general(a, b, (((1,), (1,)), ((), ())), preferred_element_type=F32)


def _tn_dot(a, b):
    return lax.dot_general(a, b, (((0,), (0,)), ((), ())), preferred_element_type=F32)


def _rope_table_kernel(cos_ref, sin_ref, *, half):
    rows = cos_ref.shape[0]
    r0 = pl.program_id(0) * rows
    pos = (lax.broadcasted_iota(jnp.int32, (rows, LANES), 0) + r0).astype(F32)
    lane = lax.broadcasted_iota(jnp.int32, (rows, LANES), 1)
    fi = (lane % half).astype(F32)
    inv_freq = jnp.exp(-(fi / half) * jnp.log(ROPE_THETA))
    ang = pos * inv_freq
    first = (lane % (2 * half)) < half
    cos_ref[...] = jnp.cos(ang)
    sin_ref[...] = jnp.where(first, -jnp.sin(ang), jnp.sin(ang))


def _rope_tables(seq, half):
    rows = min(seq, 1024)
    return pl.pallas_call(
        functools.partial(_rope_table_kernel, half=half),
        grid=(seq // rows,),
        out_specs=[pl.BlockSpec((rows, LANES), lambda i: (i, 0))] * 2,
        out_shape=[jax.ShapeDtypeStruct((seq, LANES), F32)] * 2,
        compiler_params=_cparams(("arbitrary",)),
        name="rope_tables",
    )()


def _norm_proj_kernel(x_ref, g_ref, w_ref, *rest, with_aux):
    if with_aux:
        waux_ref, o_ref, aux_ref, h_scr = rest
    else:
        o_ref, h_scr = rest
    j = pl.program_id(1)

    @pl.when(j == 0)
    def _():
        x = x_ref[...]
        ms = jnp.mean(x * x, axis=-1, keepdims=True)
        h = x * lax.rsqrt(ms + EPS) * g_ref[...]
        h_scr[...] = h.astype(BF16)
        if with_aux:
            aux_ref[...] = jnp.dot(h_scr[...], waux_ref[...], preferred_element_type=F32)

    o_ref[...] = jnp.dot(h_scr[...], w_ref[...], preferred_element_type=F32).astype(o_ref.dtype)


def _norm_proj(x, gain, w, w_aux=None, *, tm=1024, tn=1024):
    n, d = x.shape
    n_out = w.shape[1]
    tm = min(tm, n)
    tn = min(tn, n_out)
    with_aux = w_aux is not None
    in_specs = [
        pl.BlockSpec((tm, d), lambda i, j: (i, 0)),
        pl.BlockSpec((1, d), lambda i, j: (0, 0)),
        pl.BlockSpec((d, tn), lambda i, j: (0, j)),
    ]
    out_specs = [pl.BlockSpec((tm, tn), lambda i, j: (i, j))]
    out_shape = [jax.ShapeDtypeStruct((n, n_out), BF16)]
    args = [x, gain.reshape(1, d), w]
    if with_aux:
        in_specs.append(pl.BlockSpec((d, LANES), lambda i, j: (0, 0)))
        out_specs.append(pl.BlockSpec((tm, LANES), lambda i, j: (i, 0)))
        out_shape.append(jax.ShapeDtypeStruct((n, LANES), F32))
        args.append(w_aux)
    res = pl.pallas_call(
        functools.partial(_norm_proj_kernel, with_aux=with_aux),
        grid=(n // tm, n_out // tn),
        in_specs=in_specs,
        out_specs=out_specs,
        out_shape=out_shape,
        scratch_shapes=[pltpu.VMEM((tm, d), BF16)],
        compiler_params=_cparams(("parallel", "arbitrary")),
        name="norm_proj",
    )(*args)
    return res if with_aux else res[0]


def _rope128(x, cos, sin):
    return x * cos + pltpu.roll(x, RET_DK // 2, 1) * sin


def _retention_kernel(q_ref, k_ref, v_ref, g_ref, cos_ref, sin_ref, lg_ref, gn_ref, o_ref,
                      state, dmat, qdec, kdec):
    t = pl.program_id(2)
    tile = q_ref.shape[0]
    hb = RET_HEADS_PER_STEP
    dot = functools.partial(jnp.dot, preferred_element_type=F32)
    lgs = [lg_ref[hh][:, :1] for hh in range(hb)]

    @pl.when(t == 0)
    def _():
        state[...] = jnp.zeros_like(state)
        ri = lax.broadcasted_iota(jnp.int32, (tile, tile), 0)
        ci = lax.broadcasted_iota(jnp.int32, (tile, tile), 1)
        causal = ri >= ci
        rel = jnp.where(causal, ri - ci, 0).astype(F32)
        pos = lax.broadcasted_iota(jnp.int32, (tile, RET_DK), 0).astype(F32)
        for hh, lg in enumerate(lgs):
            dmat[hh] = jnp.where(causal, jnp.exp(lg * rel), 0.0)
            qdec[hh] = jnp.exp(lg * (pos + 1.0))
            kdec[hh] = jnp.exp(lg * (tile - 1.0 - pos))

    cos = cos_ref[...]
    sin = sin_ref[...]
    heads = range(hb)
    qs = [_rope128(q_ref[:, hh * RET_DK:(hh + 1) * RET_DK].astype(F32), cos, sin) for hh in heads]
    ks = [_rope128(k_ref[:, hh * RET_DK:(hh + 1) * RET_DK].astype(F32), cos, sin) * (RET_DK ** -0.5)
          for hh in heads]
    vs = [v_ref[:, hh * RET_DV:(hh + 1) * RET_DV] for hh in heads]
    ss = [_nt_dot(qs[hh].astype(BF16), ks[hh].astype(BF16)) for hh in heads]
    sts = [state[hh] for hh in heads]
    cross = [dot((qs[hh] * qdec[hh]).astype(BF16), sts[hh].astype(BF16)) for hh in heads]
    upd = [_tn_dot((ks[hh] * kdec[hh]).astype(BF16), vs[hh]) for hh in heads]
    for hh in heads:
        state[hh] = sts[hh] * jnp.exp(lgs[hh] * float(tile)) + upd[hh]
    for hh in heads:
        o = dot((ss[hh] * dmat[hh]).astype(BF16), vs[hh]) + cross[hh]
        ms = jnp.mean(o * o, axis=-1, keepdims=True)
        y = o * lax.rsqrt(ms + EPS) * gn_ref[...]
        gate = g_ref[:, hh * RET_DV:(hh + 1) * RET_DV].astype(F32)
        o_ref[:, hh * RET_DV:(hh + 1) * RET_DV] = (y * (gate * _sigmoid(gate))).astype(o_ref.dtype)


def _retention(proj, cos, sin, log_gamma, ret_norm, batch, seq):
    tile = min(RET_TILE, seq)
    nt = seq // tile
    hb = RET_HEADS_PER_STEP
    dk, dv = hb * RET_DK, hb * RET_DV
    groups = RET_HEADS // hb
    qb = 0
    kb = groups
    vb = (2 * RET_HEADS * RET_DK) // dv
    gb = vb + groups
    row = lambda b, h, t: b * nt + t
    return pl.pallas_call(
        _retention_kernel,
        grid=(batch, groups, nt),
        in_specs=[
            pl.BlockSpec((tile, dk), lambda b, h, t: (row(b, h, t), qb + h)),
            pl.BlockSpec((tile, dk), lambda b, h, t: (row(b, h, t), kb + h)),
            pl.BlockSpec((tile, dv), lambda b, h, t: (row(b, h, t), vb + h)),
            pl.BlockSpec((tile, dv), lambda b, h, t: (row(b, h, t), gb + h)),
            pl.BlockSpec((tile, LANES), lambda b, h, t: (t, 0)),
            pl.BlockSpec((tile, LANES), lambda b, h, t: (t, 0)),
            pl.BlockSpec((hb, 1, LANES), lambda b, h, t: (h, 0, 0)),
            pl.BlockSpec((1, RET_DV), lambda b, h, t: (0, 0)),
        ],
        out_specs=pl.BlockSpec((tile, dv), lambda b, h, t: (row(b, h, t), h)),
        out_shape=jax.ShapeDtypeStruct((batch * seq, RET_HEADS * RET_DV), BF16),
        scratch_shapes=[
            pltpu.VMEM((hb, RET_DK, RET_DV), F32),
            pltpu.VMEM((hb, tile, tile), F32),
            pltpu.VMEM((hb, tile, RET_DK), F32),
            pltpu.VMEM((hb, tile, RET_DK), F32),
        ],
        compiler_params=_cparams(("parallel", "parallel", "arbitrary")),
        name="retention",
    )(proj, proj, proj, proj, cos, sin, log_gamma, ret_norm.reshape(1, RET_DV))


def _conv_silu(xbuf, x_ref, w_ref):
    tile = x_ref.shape[0]
    xbuf[SUBLANES:SUBLANES + tile, :] = x_ref[...].astype(F32)
    w = w_ref[...]
    base = SUBLANES - (CONV_WIDTH - 1)
    acc = xbuf[base:base + tile, :] * w[0:1, :]
    for j in range(1, CONV_WIDTH):
        acc = acc + xbuf[base + j:base + j + tile, :] * w[j:j + 1, :]
    xbuf[0:SUBLANES, :] = xbuf[tile:tile + SUBLANES, :]
    return acc * _sigmoid(acc)


def _l2norm(x):
    return x * lax.rsqrt(jnp.sum(x * x, axis=-1, keepdims=True) + EPS)


def _softplus(x):
    return jnp.maximum(x, 0.0) + jnp.log1p(jnp.exp(-jnp.abs(x)))


def _unit_lower_inverses(lows):
    c = lows[0].shape[0]
    dot = functools.partial(jnp.dot, preferred_element_type=F32)
    eye = (lax.broadcasted_iota(jnp.int32, (c, c), 0) == lax.broadcasted_iota(jnp.int32, (c, c), 1)).astype(F32)
    invs = [eye - low for low in lows]
    powers = [_split_bf16(low) for low in lows]
    span = 2
    while span < c:
        powers = [_split_bf16(_dot_split(hi, lo, hi, lo)) for hi, lo in powers]
        invs = [inv + _dot_split(*_split_bf16(inv), hi, lo) for inv, (hi, lo) in zip(invs, powers)]
        span *= 2
    return invs


def _split_bf16(a):
    hi = a.astype(BF16)
    return hi, (a - hi.astype(F32)).astype(BF16)


def _dot_split(a_hi, a_lo, b_hi, b_lo):
    dot = functools.partial(jnp.dot, preferred_element_type=F32)
    return dot(a_hi, b_hi) + dot(a_hi, b_lo) + dot(a_lo, b_hi)


def _chunk_cumsum(tril_b, g):
    dot = functools.partial(jnp.dot, preferred_element_type=F32)
    g1 = g.astype(BF16)
    r1 = g - g1.astype(F32)
    g2 = r1.astype(BF16)
    g3 = (r1 - g2.astype(F32)).astype(BF16)
    return dot(tril_b, g1) + dot(tril_b, g2) + dot(tril_b, g3)


def _gdn_kernel(xq_ref, xk_ref, xv_ref, z_ref, aux_ref, wq_ref, wk_ref, wv_ref, alog_ref, dtb_ref,
                gn_ref, o_ref, state, qbuf, kbuf, vbuf):
    hp = pl.program_id(1)
    t = pl.program_id(2)
    tile = xq_ref.shape[0]
    c = GDN_CHUNK
    hb = GDN_HEADS_PER_STEP

    @pl.when(t == 0)
    def _():
        state[...] = jnp.zeros_like(state)
        qbuf[0:SUBLANES, :] = jnp.zeros((SUBLANES, qbuf.shape[1]), F32)
        kbuf[0:SUBLANES, :] = jnp.zeros((SUBLANES, kbuf.shape[1]), F32)
        vbuf[0:SUBLANES, :] = jnp.zeros((SUBLANES, vbuf.shape[1]), F32)

    q_all = _conv_silu(qbuf, xq_ref, wq_ref)
    k_all = _conv_silu(kbuf, xk_ref, wk_ref)
    v_all = _conv_silu(vbuf, xv_ref, wv_ref)

    aux = aux_ref[...]
    lane = lax.broadcasted_iota(jnp.int32, (tile, LANES), 1)
    beta_all = _sigmoid(aux)
    g_all = -jnp.exp(alog_ref[...]) * _softplus(aux + dtb_ref[...])

    ri = lax.broadcasted_iota(jnp.int32, (c, c), 0)
    ci = lax.broadcasted_iota(jnp.int32, (c, c), 1)
    causal = ri >= ci
    strict = ri > ci
    rt = lax.broadcasted_iota(jnp.int32, (tile, tile), 0)
    ct = lax.broadcasted_iota(jnp.int32, (tile, tile), 1)
    same_chunk = lax.shift_right_logical(rt, c.bit_length() - 1) == lax.shift_right_logical(ct, c.bit_length() - 1)
    gcum_all = _chunk_cumsum(((rt >= ct) & same_chunk).astype(BF16), g_all)

    dot = functools.partial(jnp.dot, preferred_element_type=F32)
    nchunk = tile // c
    pairs = []
    for hh in range(hb):
        h = hp * hb + hh
        q = _l2norm(q_all[:, hh * GDN_DK:(hh + 1) * GDN_DK]) * (GDN_DK ** -0.5)
        k = _l2norm(k_all[:, hh * GDN_DK:(hh + 1) * GDN_DK])
        v = v_all[:, hh * GDN_DV:(hh + 1) * GDN_DV]
        beta = jnp.sum(jnp.where(lane == h, beta_all, 0.0), axis=1, keepdims=True)
        gc = jnp.sum(jnp.where(lane == h + GDN_HEADS, gcum_all, 0.0), axis=1, keepdims=True)
        for i in range(nchunk):
            sl = slice(i * c, (i + 1) * c)
            qc, kc, vc, bc = q[sl], k[sl], v[sl], beta[sl]
            gcum = jnp.broadcast_to(gc[sl], (c, c))
            rel = gcum - gcum.T
            decay = jnp.where(causal, jnp.exp(jnp.where(causal, rel, 0.0)), 0.0)
            eg = jnp.exp(gcum)
            kb = kc * bc
            kcb = kc.astype(BF16)
            g_last = gcum[c - 1:c, :]
            pairs.append(dict(
                hh=hh, i=i,
                low=jnp.where(strict, _nt_dot(kb.astype(BF16), kcb) * decay, 0.0),
                attn=jnp.where(causal, _nt_dot(qc.astype(BF16), kcb) * decay, 0.0).astype(BF16),
                vb=(vc * bc).astype(BF16),
                kbe=(kb * eg).astype(BF16),
                qg=(qc * eg).astype(BF16),
                k_tail_t=(kc * jnp.exp(g_last - gcum)).T.astype(BF16),
                sdec=jnp.exp(g_last[:, :1]),
            ))
    pairs.sort(key=lambda p: (p["i"], p["hh"]))
    invs = _unit_lower_inverses([p["low"] for p in pairs])
    for p, inv in zip(pairs, invs):
        inv_b = inv.astype(BF16)
        p["u"] = dot(inv_b, p["vb"])
        p["w"] = dot(inv_b, p["kbe"]).astype(BF16)
    for p in pairs:
        hh, sl = p["hh"], slice(p["i"] * c, (p["i"] + 1) * c)
        st = state[hh]
        stb = st.astype(BF16)
        v_new = (p["u"] - dot(p["w"], stb)).astype(BF16)
        state[hh] = st * p["sdec"] + dot(p["k_tail_t"], v_new)
        o = dot(p["qg"], stb) + dot(p["attn"], v_new)
        ms = jnp.mean(o * o, axis=-1, keepdims=True)
        y = o * lax.rsqrt(ms + EPS) * gn_ref[...]
        z = z_ref[sl, hh * GDN_DV:(hh + 1) * GDN_DV].astype(F32)
        o_ref[sl, hh * GDN_DV:(hh + 1) * GDN_DV] = (y * (z * _sigmoid(z))).astype(o_ref.dtype)


def _gdn(proj, aux, conv_w, alog_row, dtb_row, gdn_norm, batch, seq):
    tile = min(GDN_TILE, seq)
    nt = seq // tile
    hb = GDN_HEADS_PER_STEP
    dk, dv = hb * GDN_DK, hb * GDN_DV
    base = 2 * RET_HEADS * RET_DK + 2 * RET_HEADS * RET_DV
    qb = base // dk
    kb = qb + GDN_HEADS // hb
    vb = (base + 2 * GDN_HEADS * GDN_DK) // dv
    zb = vb + GDN_HEADS // hb
    cvb = (2 * GDN_HEADS * GDN_DK) // dv
    row = lambda b, h, t: b * nt + t
    return pl.pallas_call(
        _gdn_kernel,
        grid=(batch, GDN_HEADS // hb, nt),
        in_specs=[
            pl.BlockSpec((tile, dk), lambda b, h, t: (row(b, h, t), qb + h)),
            pl.BlockSpec((tile, dk), lambda b, h, t: (row(b, h, t), kb + h)),
            pl.BlockSpec((tile, dv), lambda b, h, t: (row(b, h, t), vb + h)),
            pl.BlockSpec((tile, dv), lambda b, h, t: (row(b, h, t), zb + h)),
            pl.BlockSpec((tile, LANES), lambda b, h, t: (row(b, h, t), 0)),
            pl.BlockSpec((CONV_WIDTH, dk), lambda b, h, t: (0, h)),
            pl.BlockSpec((CONV_WIDTH, dk), lambda b, h, t: (0, GDN_HEADS // hb + h)),
            pl.BlockSpec((CONV_WIDTH, dv), lambda b, h, t: (0, cvb + h)),
            pl.BlockSpec((1, LANES), lambda b, h, t: (0, 0)),
            pl.BlockSpec((1, LANES), lambda b, h, t: (0, 0)),
            pl.BlockSpec((1, GDN_DV), lambda b, h, t: (0, 0)),
        ],
        out_specs=pl.BlockSpec((tile, dv), lambda b, h, t: (row(b, h, t), h)),
        out_shape=jax.ShapeDtypeStruct((batch * seq, GDN_HEADS * GDN_DV), BF16),
        scratch_shapes=[
            pltpu.VMEM((hb, GDN_DK, GDN_DV), F32),
            pltpu.VMEM((tile + SUBLANES, dk), F32),
            pltpu.VMEM((tile + SUBLANES, dk), F32),
            pltpu.VMEM((tile + SUBLANES, dv), F32),
        ],
        compiler_params=_cparams(("parallel", "parallel", "arbitrary")),
        name="gated_deltanet",
    )(proj, proj, proj, proj, aux, conv_w, conv_w, conv_w, alog_row, dtb_row, gdn_norm.reshape(1, GDN_DV))


def _rope64(x, cos, sin):
    half = SWA_HEAD_DIM // 2
    lane = lax.broadcasted_iota(jnp.int32, x.shape, x.ndim - 1)
    first = (lane % SWA_HEAD_DIM) < half
    rot = jnp.where(first, pltpu.roll(x, LANES - half, x.ndim - 1), pltpu.roll(x, half, x.ndim - 1))
    return x * cos + rot * sin


def _head_rmsnorm(x, gain, seg_ones):
    sq_hi, sq_lo = _split_bf16(x * x)
    ss = jnp.dot(sq_hi, seg_ones, preferred_element_type=F32) + jnp.dot(sq_lo, seg_ones, preferred_element_type=F32)
    return x * lax.rsqrt(ss * (1.0 / SWA_HEAD_DIM) + EPS) * gain


def _swa_kernel(sink_ref, q_ref, k_ref, v_ref, cos_ref, sin_ref, qn_ref, kn_ref, seg_ref, o_ref,
                kprev, vprev):
    i = pl.program_id(1)
    w = WINDOW
    pairs = SWA_KV_HEADS // 2
    grp = SWA_Q_HEADS // SWA_KV_HEADS
    qchunks = grp * SWA_HEAD_DIM // LANES

    @pl.when(i == 0)
    def _():
        kprev[...] = jnp.zeros_like(kprev)
        vprev[...] = jnp.zeros_like(vprev)

    cos = cos_ref[...]
    sin = sin_ref[...]
    seg = seg_ref[...]
    lane = lax.broadcasted_iota(jnp.int32, (w, LANES), 1)
    lo_mask = lane < SWA_HEAD_DIM

    k_lo, k_hi, v_lo, v_hi = [], [], [], []
    for c in range(pairs):
        kc = k_ref[:, c * LANES:(c + 1) * LANES].astype(F32)
        kc = _rope64(_head_rmsnorm(kc, kn_ref[...], seg), cos, sin)
        vc = v_ref[:, c * LANES:(c + 1) * LANES].astype(F32)
        for src, lo_list, hi_list in ((kc, k_lo, k_hi), (vc, v_lo, v_hi)):
            a_lo = jnp.where(lo_mask, src, 0.0)
            b_hi = jnp.where(lo_mask, 0.0, src)
            lo_list += [a_lo, pltpu.roll(b_hi, SWA_HEAD_DIM, 1)]
            hi_list += [pltpu.roll(a_lo, SWA_HEAD_DIM, 1), b_hi]

    qpos = lax.broadcasted_iota(jnp.int32, (w, 2 * w), 0) + w
    kpos = lax.broadcasted_iota(jnp.int32, (w, 2 * w), 1)
    rel = qpos - kpos
    first_key = jnp.where(i > 0, 0, w)
    valid = (rel >= 0) & (rel < w) & (kpos >= first_key)
    scale = SWA_HEAD_DIM ** -0.5 * LOG2E
    cos_q = jnp.concatenate([cos] * qchunks, axis=0)
    sin_q = jnp.concatenate([sin] * qchunks, axis=0)

    for kvh in range(SWA_KV_HEADS):
        q0 = kvh * grp * SWA_HEAD_DIM
        q2 = jnp.concatenate(
            [q_ref[:, q0 + c * LANES:q0 + (c + 1) * LANES].astype(F32) for c in range(qchunks)], axis=0)
        q2 = _head_rmsnorm(q2, qn_ref[...], seg)
        q2 = (_rope64(q2, cos_q, sin_q) * scale).astype(BF16)
        kl = jnp.concatenate([kprev[0, kvh], k_lo[kvh].astype(BF16)], axis=0)
        kh = jnp.concatenate([kprev[1, kvh], k_hi[kvh].astype(BF16)], axis=0)
        vl = jnp.concatenate([vprev[0, kvh], v_lo[kvh].astype(BF16)], axis=0)
        vh = jnp.concatenate([vprev[1, kvh], v_hi[kvh].astype(BF16)], axis=0)
        acc = None
        for par, kk, vv in ((0, kl, vl), (1, kh, vh)):
            sc = _nt_dot(q2, kk)
            probs, denoms = [], []
            for c in range(qchunks):
                sink = sink_ref[kvh * grp + 2 * c + par] * LOG2E
                s = jnp.where(valid, sc[c * w:(c + 1) * w, :], -jnp.inf)
                mx = jnp.maximum(jnp.max(s, axis=-1, keepdims=True), sink)
                p = jnp.exp2(s - mx)
                denoms.append(jnp.sum(p, axis=-1, keepdims=True) + jnp.exp2(sink - mx))
                probs.append(p.astype(BF16))
            pv = jnp.dot(jnp.concatenate(probs, axis=0), vv, preferred_element_type=F32)
            pv = [pv[c * w:(c + 1) * w, :] * (1.0 / denoms[c]) for c in range(qchunks)]
            acc = pv if acc is None else [a + b for a, b in zip(acc, pv)]
        for c in range(qchunks):
            o_ref[:, q0 + c * LANES:q0 + (c + 1) * LANES] = acc[c].astype(o_ref.dtype)

    for kvh in range(SWA_KV_HEADS):
        kprev[0, kvh] = k_lo[kvh].astype(BF16)
        kprev[1, kvh] = k_hi[kvh].astype(BF16)
        vprev[0, kvh] = v_lo[kvh].astype(BF16)
        vprev[1, kvh] = v_hi[kvh].astype(BF16)


def _swa(proj, cos, sin, q_norm, k_norm, sinks, batch, seq):
    w = WINDOW
    nb = seq // w
    qw = SWA_Q_HEADS * SWA_HEAD_DIM
    kvw = SWA_KV_HEADS * SWA_HEAD_DIM
    kblk = qw // kvw
    reps = LANES // SWA_HEAD_DIM
    seg = (jnp.arange(LANES)[:, None] // SWA_HEAD_DIM == jnp.arange(LANES)[None, :] // SWA_HEAD_DIM).astype(BF16)
    return pl.pallas_call(
        _swa_kernel,
        grid=(batch, nb),
        in_specs=[
            pl.BlockSpec(memory_space=pltpu.SMEM),
            pl.BlockSpec((w, qw), lambda b, i: (b * nb + i, 0)),
            pl.BlockSpec((w, kvw), lambda b, i: (b * nb + i, kblk)),
            pl.BlockSpec((w, kvw), lambda b, i: (b * nb + i, kblk + 1)),
            pl.BlockSpec((w, LANES), lambda b, i: (i, 0)),
            pl.BlockSpec((w, LANES), lambda b, i: (i, 0)),
            pl.BlockSpec((1, LANES), lambda b, i: (0, 0)),
            pl.BlockSpec((1, LANES), lambda b, i: (0, 0)),
            pl.BlockSpec((LANES, LANES), lambda b, i: (0, 0)),
        ],
        out_specs=pl.BlockSpec((w, qw), lambda b, i: (b * nb + i, 0)),
        out_shape=jax.ShapeDtypeStruct((batch * seq, qw), BF16),
        scratch_shapes=[
            pltpu.VMEM((2, SWA_KV_HEADS, w, LANES), BF16),
            pltpu.VMEM((2, SWA_KV_HEADS, w, LANES), BF16),
        ],
        compiler_params=_cparams(("parallel", "arbitrary")),
        name="swa",
    )(sinks.astype(F32), proj, proj, proj, cos, sin,
      jnp.tile(q_norm.astype(F32), reps).reshape(1, LANES),
      jnp.tile(k_norm.astype(F32), reps).reshape(1, LANES), seg)


def _route(logits):
    tm = logits.shape[0]
    lane = lax.broadcasted_iota(jnp.int32, (tm, LANES), 1)
    lane_f = lane.astype(F32)
    big = float(LANES)
    neg = -jnp.inf
    gl = jnp.where(lane < N_GROUPS, logits, neg)
    gmax = jnp.max(gl, axis=1, keepdims=True)
    gidx = jnp.min(jnp.where(gl == gmax, lane_f, big), axis=1, keepdims=True)
    g_p = 1.0 / jnp.sum(jnp.exp(gl - gmax), axis=1, keepdims=True)
    egroup = lax.shift_right_arithmetic(lane - N_GROUPS, 3).astype(F32)
    in_group = (lane >= N_GROUPS) & (lane < N_GROUPS + N_EXPERTS) & (egroup == gidx)
    el = jnp.where(in_group, logits, neg)
    emax = jnp.max(el, axis=1, keepdims=True)
    ee = jnp.exp(el - emax)
    prob = ee / jnp.sum(ee, axis=1, keepdims=True)
    pm = jnp.where(in_group, prob, -1.0)
    p1 = jnp.max(pm, axis=1, keepdims=True)
    i1 = jnp.min(jnp.where(pm == p1, lane_f, big), axis=1, keepdims=True)
    pm2 = jnp.where(lane_f == i1, -1.0, pm)
    p2 = jnp.max(pm2, axis=1, keepdims=True)
    i2 = jnp.min(jnp.where(pm2 == p2, lane_f, big), axis=1, keepdims=True)
    den = p1 + p2
    out = jnp.where(lane == 0, i1 - N_GROUPS, 0.0)
    out = jnp.where(lane == 1, i2 - N_GROUPS, out)
    out = jnp.where(lane == 2, g_p * p1 / den, out)
    out = jnp.where(lane == 3, g_p * p2 / den, out)
    return out


def _out_proj_kernel(*refs, n_act):
    acts = refs[:n_act]
    ws = refs[n_act:2 * n_act]
    x_ref, g_ref, wr_ref, x1_ref, h_ref, route_ref = refs[2 * n_act:]
    y = x_ref[...]
    for a, w in zip(acts, ws):
        y = y + jnp.dot(a[...], w[...], preferred_element_type=F32)
    x1_ref[...] = y
    ms = jnp.mean(y * y, axis=-1, keepdims=True)
    h = y * lax.rsqrt(ms + EPS) * g_ref[...]
    tm = y.shape[0]
    for s in range(ROW_TILES):
        h_ref[pl.ds(s, tm, stride=ROW_TILES), :] = h[:, s * LANES:(s + 1) * LANES]
    h_hi, h_lo = _split_bf16(h)
    route_ref[...] = _route(_dot_split(h_hi, h_lo, wr_ref[0], wr_ref[1]))


def _out_proj(acts, ws, x, gain, w_router, *, tm=256):
    n, d = x.shape
    tm = min(tm, n)
    n_act = len(acts)
    const = lambda i: (0, 0)
    in_specs = [pl.BlockSpec((tm, a.shape[1]), lambda i: (i, 0)) for a in acts]
    in_specs += [pl.BlockSpec(w.shape, const, pipeline_mode=pl.Buffered(1)) for w in ws]
    in_specs += [
        pl.BlockSpec((tm, d), lambda i: (i, 0)),
        pl.BlockSpec((1, d), const),
        pl.BlockSpec((2, d, LANES), lambda i: (0, 0, 0)),
    ]
    return pl.pallas_call(
        functools.partial(_out_proj_kernel, n_act=n_act),
        grid=(n // tm,),
        in_specs=in_specs,
        out_specs=[
            pl.BlockSpec((tm, d), lambda i: (i, 0)),
            pl.BlockSpec((tm * ROW_TILES, LANES), lambda i: (i, 0)),
            pl.BlockSpec((tm, LANES), lambda i: (i, 0)),
        ],
        out_shape=[
            jax.ShapeDtypeStruct((n, d), F32),
            jax.ShapeDtypeStruct((n * ROW_TILES, LANES), F32),
            jax.ShapeDtypeStruct((n, LANES), F32),
        ],
        compiler_params=_cparams(("parallel",)),
        name="out_proj_router",
    )(*acts, *ws, x, gain.reshape(1, d), w_router)


def _row_gather_start(idx_ref, base, count, src_hbm, dst, sem, *, both_queues):
    group = 8

    def body(gi, carry):
        for j in range(group):
            r = gi * group + j
            tok = idx_ref[base + r]
            pltpu.make_async_copy(src_hbm.at[pl.ds(pl.multiple_of(tok * ROW_TILES, ROW_TILES), ROW_TILES)],
                                  dst.at[:, pl.ds(r, 1), :],
                                  sem).start(priority=j % 2 if both_queues else 0)
        return carry
    lax.fori_loop(0, count // group, body, 0)


def _row_gather_wait(dst, sem):
    pltpu.make_async_copy(dst, dst, sem).wait()


def _expert_kernel(be_ref, nu_ref, nxt_ref, tok_ref, h_hbm, wg_hbm, wu_hbm, wd_hbm, o_ref,
                   xbuf, stage_g, stage_u, stage_d, wg_b, wu_b, wd_b, gsem, wsem, *, layer):
    i = pl.program_id(0)
    bm = MOE_BM
    nu = nu_ref[0]
    slot = i % 2
    e = be_ref[i]
    weight_queue = 1

    def weight_copies(expert):
        return (pltpu.make_async_copy(wg_hbm.at[layer, expert], stage_g, wsem.at[0]),
                pltpu.make_async_copy(wu_hbm.at[layer, expert], stage_u, wsem.at[1]),
                pltpu.make_async_copy(wd_hbm.at[layer, expert], stage_d, wsem.at[2]))

    @pl.when(i == 0)
    def _():
        _row_gather_start(tok_ref, 0, bm, h_hbm, xbuf.at[0], gsem.at[0], both_queues=False)
        for cp in weight_copies(e):
            cp.start(priority=weight_queue)

    @pl.when(i < nu)
    def _():
        @pl.when(i + 1 < nu)
        def _():
            _row_gather_start(tok_ref, (i + 1) * bm, bm, h_hbm, xbuf.at[1 - slot], gsem.at[1 - slot],
                              both_queues=False)

        first_of_run = jnp.logical_or(i == 0, e != be_ref[jnp.maximum(i - 1, 0)])

        @pl.when(first_of_run)
        def _():
            for cp, stage, dst in zip(weight_copies(e), (stage_g, stage_u, stage_d), (wg_b, wu_b, wd_b)):
                cp.wait()
                dst[...] = stage[...].astype(BF16)
            nxt = nxt_ref[e]

            @pl.when(nxt >= 0)
            def _():
                for cp in weight_copies(nxt):
                    cp.start(priority=weight_queue)

        _row_gather_wait(xbuf.at[slot], gsem.at[slot])
        x = jnp.concatenate([xbuf[slot, s].astype(BF16) for s in range(ROW_TILES)], axis=1)
        g = jnp.dot(x, wg_b[...], preferred_element_type=F32)
        u = jnp.dot(x, wu_b[...], preferred_element_type=F32)
        hid = (g * _sigmoid(g) * u).astype(BF16)
        y = jnp.dot(hid, wd_b[...], preferred_element_type=F32)
        for s in range(ROW_TILES):
            o_ref[pl.ds(s, bm, stride=ROW_TILES), :] = y[:, s * LANES:(s + 1) * LANES]

    @pl.when(i >= nu)
    def _():
        o_ref[...] = jnp.zeros_like(o_ref)


def _expert_mlp(h_tiled, tok_pad, block_expert, n_used, next_expert, w_gate, w_up, w_down, layer):
    n_pad = tok_pad.shape[0]
    bm = MOE_BM
    _, _, d, ff = w_gate.shape
    any_spec = pl.BlockSpec(memory_space=pl.ANY)
    grid_spec = pltpu.PrefetchScalarGridSpec(
        num_scalar_prefetch=4,
        grid=(n_pad // bm,),
        in_specs=[any_spec, any_spec, any_spec, any_spec],
        out_specs=pl.BlockSpec((bm * ROW_TILES, LANES), lambda i, *_: (i, 0)),
        scratch_shapes=[
            pltpu.VMEM((2, ROW_TILES, bm, LANES), F32),
            pltpu.VMEM((d, ff), F32), pltpu.VMEM((d, ff), F32), pltpu.VMEM((ff, d), F32),
            pltpu.VMEM((d, ff), BF16), pltpu.VMEM((d, ff), BF16), pltpu.VMEM((ff, d), BF16),
            pltpu.SemaphoreType.DMA((2,)),
            pltpu.SemaphoreType.DMA((3,)),
        ],
    )
    return pl.pallas_call(
        functools.partial(_expert_kernel, layer=layer),
        grid_spec=grid_spec,
        out_shape=jax.ShapeDtypeStruct((n_pad * ROW_TILES, LANES), F32),
        compiler_params=_cparams(("arbitrary",)),
        name="expert_mlp",
    )(block_expert, n_used, next_expert, tok_pad, h_tiled.reshape(-1, 1, LANES), w_gate, w_up, w_down)


def _combine_kernel(dest_ref, x1_ref, route_ref, y_hbm, o_ref, ybuf, sem):
    i = pl.program_id(0)
    nsteps = pl.num_programs(0)
    tm = x1_ref.shape[0]
    slot = i % 2

    def start(step, sl):
        for k in range(2):
            _row_gather_start(dest_ref, k * (nsteps * tm) + step * tm, tm, y_hbm, ybuf.at[sl, k], sem.at[sl],
                              both_queues=True)

    @pl.when(i == 0)
    def _():
        start(0, 0)

    @pl.when(i + 1 < nsteps)
    def _():
        start(i + 1, 1 - slot)

    _row_gather_wait(ybuf.at[slot], sem.at[slot])
    route = route_ref[...]
    g0 = jnp.broadcast_to(route[:, 2:3], (tm, LANES))
    g1 = jnp.broadcast_to(route[:, 3:4], (tm, LANES))
    for s in range(ROW_TILES):
        o_ref[:, s * LANES:(s + 1) * LANES] = (x1_ref[:, s * LANES:(s + 1) * LANES]
                                               + g0 * ybuf[slot, 0, s] + g1 * ybuf[slot, 1, s])


def _combine(x1, route, dest, y_tiled, *, tm=256):
    n, d = x1.shape
    tm = min(tm, n)
    grid_spec = pltpu.PrefetchScalarGridSpec(
        num_scalar_prefetch=1,
        grid=(n // tm,),
        in_specs=[
            pl.BlockSpec((tm, d), lambda i, *_: (i, 0)),
            pl.BlockSpec((tm, LANES), lambda i, *_: (i, 0)),
            pl.BlockSpec(memory_space=pl.ANY),
        ],
        out_specs=pl.BlockSpec((tm, d), lambda i, *_: (i, 0)),
        scratch_shapes=[
            pltpu.VMEM((2, 2, ROW_TILES, tm, LANES), F32),
            pltpu.SemaphoreType.DMA((2,)),
        ],
    )
    return pl.pallas_call(
        _combine_kernel,
        grid_spec=grid_spec,
        out_shape=jax.ShapeDtypeStruct((n, d), F32),
        compiler_params=_cparams(("arbitrary",)),
        name="moe_combine",
    )(dest, x1, route, y_tiled.reshape(-1, 1, LANES))


def _moe(x1, h_tiled, route, w_gate, w_up, w_down, layer):
    n, d = x1.shape
    bm = MOE_BM
    n_assign = 2 * n
    expert_id = route[:, 0:2].astype(jnp.int32).reshape(-1)
    onehot = (expert_id[:, None] == jnp.arange(N_EXPERTS, dtype=jnp.int32)[None, :]).astype(jnp.int32)
    csum = jnp.cumsum(onehot, axis=0)
    rank = jnp.sum(csum * onehot, axis=1) - 1
    counts = csum[-1]
    padded = (counts + bm - 1) // bm * bm
    pend = jnp.cumsum(padded)
    pstart = pend - padded
    dest = (jnp.sum(onehot * pstart[None, :], axis=1) + rank).astype(jnp.int32)
    n_pad = (n_assign + N_EXPERTS * (bm - 1) + bm - 1) // bm * bm
    n_blk = n_pad // bm
    token_id = jnp.arange(n_assign, dtype=jnp.int32) // 2
    tok_pad = (jnp.arange(n_pad, dtype=jnp.int32) % n).at[dest].set(token_id)
    block_start = jnp.arange(n_blk, dtype=jnp.int32) * bm
    block_expert = jnp.minimum(jnp.sum(pend[None, :] <= block_start[:, None], axis=1), N_EXPERTS - 1)
    n_used = (pend[-1] // bm).astype(jnp.int32).reshape(1)
    ids = jnp.arange(N_EXPERTS, dtype=jnp.int32)
    cand = jnp.where((counts[None, :] > 0) & (ids[None, :] > ids[:, None]), ids[None, :], N_EXPERTS)
    next_expert = jnp.min(cand, axis=1)
    next_expert = jnp.where(next_expert >= N_EXPERTS, -1, next_expert).astype(jnp.int32)
    y_tiled = _expert_mlp(h_tiled, tok_pad, block_expert.astype(jnp.int32), n_used, next_expert,
                          w_gate, w_up, w_down, layer)
    dest_by_choice = dest.reshape(n, 2).T.reshape(-1).astype(jnp.int32)
    return _combine(x1, route, dest_by_choice, y_tiled)


def _router_weights(w_group, w_expert):
    d = w_group.shape[0]
    pad = jnp.zeros((d, LANES - N_GROUPS - N_EXPERTS), F32)
    w = jnp.concatenate([w_group.astype(F32), w_expert.astype(F32), pad], axis=1)
    w_hi = w.astype(BF16)
    w_lo = (w - w_hi.astype(F32)).astype(BF16)
    return jnp.stack([w_hi, w_lo])


def kernel(x, norm_mix, norm_ffn, even_w_in, ret_norm, gdn_conv, gdn_a_log, gdn_dt_bias, gdn_norm,
           even_w_out, odd_w_in, q_norm, k_norm, attn_sinks, odd_w_out, router_group, router_expert,
           expert_w_gate, expert_w_up, expert_w_down):
    batch, seq, d = x.shape
    n = batch * seq
    xt = x.reshape(n, d)

    w_in = even_w_in[0]
    w_main = w_in[:, :EVEN_MAIN].astype(BF16)
    w_aux = jnp.pad(w_in[:, EVEN_MAIN:], ((0, 0), (0, LANES - 2 * GDN_HEADS))).astype(BF16)
    proj, aux = _norm_proj(xt, norm_mix[0], w_main, w_aux)
    cos_r, sin_r = _rope_tables(seq, RET_DK // 2)
    log_gamma = jnp.log1p(-jnp.exp2(-5.0 - jnp.arange(RET_HEADS, dtype=F32)))
    log_gamma = jnp.broadcast_to(log_gamma[:, None, None], (RET_HEADS, 1, LANES))
    o_ret = _retention(proj, cos_r, sin_r, log_gamma, ret_norm[0].astype(F32), batch, seq)
    lane_pad = (GDN_HEADS, LANES - 2 * GDN_HEADS)
    alog_row = jnp.pad(gdn_a_log[0].astype(F32), lane_pad).reshape(1, LANES)
    dtb_row = jnp.pad(gdn_dt_bias[0].astype(F32), lane_pad).reshape(1, LANES)
    o_gdn = _gdn(proj, aux, gdn_conv[0].astype(F32), alog_row, dtb_row, gdn_norm[0].astype(F32), batch, seq)
    w_out = even_w_out[0].astype(BF16)
    split = RET_HEADS * RET_DV
    x1, h2, route = _out_proj([o_ret, o_gdn], [w_out[:split], w_out[split:]], xt, norm_ffn[0],
                              _router_weights(router_group[0], router_expert[0]))
    xt = _moe(x1, h2, route, expert_w_gate, expert_w_up, expert_w_down, 0)

    proj = _norm_proj(xt, norm_mix[1], odd_w_in[0].astype(BF16), tn=ODD_IN // 2)
    cos_s, sin_s = _rope_tables(seq, SWA_HEAD_DIM // 2)
    o_swa = _swa(proj, cos_s, sin_s, q_norm[0], k_norm[0], attn_sinks[0], batch, seq)
    x1, h2, route = _out_proj([o_swa], [odd_w_out[0].astype(BF16)], xt, norm_ffn[1],
                              _router_weights(router_group[1], router_expert[1]))
    xt = _moe(x1, h2, route, expert_w_gate, expert_w_up, expert_w_down, 1)
    return xt.reshape(batch, seq, d)
```

```python
import functools

import jax
import jax.numpy as jnp
from jax import lax
from jax.experimental import pallas as pl
from jax.experimental.pallas import tpu as pltpu

F32 = jnp.float32
BF16 = jnp.bfloat16

D_MODEL = 2048
RET_HEADS = 8
RET_DK = 128
RET_DV = 256
GDN_HEADS = 8
GDN_DK = 128
GDN_DV = 256
CONV_WIDTH = 4
SWA_Q_HEADS = 32
SWA_KV_HEADS = 4
SWA_HEAD_DIM = 64
WINDOW = 128
ROPE_THETA = 10000.0
N_GROUPS = 4
EXPERTS_PER_GROUP = 8
N_EXPERTS = N_GROUPS * EXPERTS_PER_GROUP
D_FF_EXPERT = 768
EPS = 1e-6
LOG2E = 1.4426950408889634

LANES = 128
SUBLANES = 8
VMEM_LIMIT = 56 * 1024 * 1024

EVEN_MAIN = 2 * RET_HEADS * RET_DK + 2 * RET_HEADS * RET_DV + 2 * GDN_HEADS * GDN_DK + 2 * GDN_HEADS * GDN_DV
ODD_IN = (SWA_Q_HEADS + 2 * SWA_KV_HEADS) * SWA_HEAD_DIM

RET_TILE = 256
RET_HEADS_PER_STEP = 4
GDN_TILE = 256
GDN_CHUNK = 128
GDN_HEADS_PER_STEP = 8
MOE_BM = 256
ROW_TILES = D_MODEL // LANES


def _cparams(sem):
    return pltpu.CompilerParams(dimension_semantics=sem, vmem_limit_bytes=VMEM_LIMIT)


def _sigmoid(x):
    return 1.0 / (1.0 + jnp.exp(-x))


def _nt_dot(a, b):
    return lax.dot_general(a, b, (((1,), (1,)), ((), ())), preferred_element_type=F32)


def _tn_dot(a, b):
    return lax.dot_general(a, b, (((0,), (0,)), ((), ())), preferred_element_type=F32)


def _rope_table_kernel(cos_ref, sin_ref, *, half):
    rows = cos_ref.shape[0]
    r0 = pl.program_id(0) * rows
    pos = (lax.broadcasted_iota(jnp.int32, (rows, LANES), 0) + r0).astype(F32)
    lane = lax.broadcasted_iota(jnp.int32, (rows, LANES), 1)
    fi = (lane % half).astype(F32)
    inv_freq = jnp.exp(-(fi / half) * jnp.log(ROPE_THETA))
    ang = pos * inv_freq
    first = (lane % (2 * half)) < half
    cos_ref[...] = jnp.cos(ang)
    sin_ref[...] = jnp.where(first, -jnp.sin(ang), jnp.sin(ang))


def _rope_tables(seq, half):
    rows = min(seq, 1024)
    return pl.pallas_call(
        functools.partial(_rope_table_kernel, half=half),
        grid=(seq // rows,),
        out_specs=[pl.BlockSpec((rows, LANES), lambda i: (i, 0))] * 2,
        out_shape=[jax.ShapeDtypeStruct((seq, LANES), F32)] * 2,
        compiler_params=_cparams(("arbitrary",)),
        name="rope_tables",
    )()


def _norm_proj_kernel(x_ref, g_ref, w_ref, *rest, with_aux):
    if with_aux:
        waux_ref, o_ref, aux_ref, h_scr = rest
    else:
        o_ref, h_scr = rest
    j = pl.program_id(1)

    @pl.when(j == 0)
    def _():
        x = x_ref[...]
        ms = jnp.mean(x * x, axis=-1, keepdims=True)
        h = x * lax.rsqrt(ms + EPS) * g_ref[...]
        h_scr[...] = h.astype(BF16)
        if with_aux:
            aux_ref[...] = jnp.dot(h_scr[...], waux_ref[...], preferred_element_type=F32)

    o_ref[...] = jnp.dot(h_scr[...], w_ref[...], preferred_element_type=F32).astype(o_ref.dtype)


def _norm_proj(x, gain, w, w_aux=None, *, tm=1024, tn=1024):
    n, d = x.shape
    n_out = w.shape[1]
    tm = min(tm, n)
    tn = min(tn, n_out)
    with_aux = w_aux is not None
    in_specs = [
        pl.BlockSpec((tm, d), lambda i, j: (i, 0)),
        pl.BlockSpec((1, d), lambda i, j: (0, 0)),
        pl.BlockSpec((d, tn), lambda i, j: (0, j)),
    ]
    out_specs = [pl.BlockSpec((tm, tn), lambda i, j: (i, j))]
    out_shape = [jax.ShapeDtypeStruct((n, n_out), BF16)]
    args = [x, gain.reshape(1, d), w]
    if with_aux:
        in_specs.append(pl.BlockSpec((d, LANES), lambda i, j: (0, 0)))
        out_specs.append(pl.BlockSpec((tm, LANES), lambda i, j: (i, 0)))
        out_shape.append(jax.ShapeDtypeStruct((n, LANES), F32))
        args.append(w_aux)
    res = pl.pallas_call(
        functools.partial(_norm_proj_kernel, with_aux=with_aux),
        grid=(n // tm, n_out // tn),
        in_specs=in_specs,
        out_specs=out_specs,
        out_shape=out_shape,
        scratch_shapes=[pltpu.VMEM((tm, d), BF16)],
        compiler_params=_cparams(("parallel", "arbitrary")),
        name="norm_proj",
    )(*args)
    return res if with_aux else res[0]


def _rope128(x, cos, sin):
    return x * cos + pltpu.roll(x, RET_DK // 2, 1) * sin


def _retention_kernel(q_ref, k_ref, v_ref, g_ref, cos_ref, sin_ref, lg_ref, gn_ref, o_ref,
                      state, dmat, qdec, kdec):
    t = pl.program_id(2)
    tile = q_ref.shape[0]
    hb = RET_HEADS_PER_STEP
    dot = functools.partial(jnp.dot, preferred_element_type=F32)
    lgs = [lg_ref[hh][:, :1] for hh in range(hb)]

    @pl.when(t == 0)
    def _():
        state[...] = jnp.zeros_like(state)
        ri = lax.broadcasted_iota(jnp.int32, (tile, tile), 0)
        ci = lax.broadcasted_iota(jnp.int32, (tile, tile), 1)
        causal = ri >= ci
        rel = jnp.where(causal, ri - ci, 0).astype(F32)
        pos = lax.broadcasted_iota(jnp.int32, (tile, RET_DK), 0).astype(F32)
        for hh, lg in enumerate(lgs):
            dmat[hh] = jnp.where(causal, jnp.exp(lg * rel), 0.0)
            qdec[hh] = jnp.exp(lg * (pos + 1.0))
            kdec[hh] = jnp.exp(lg * (tile - 1.0 - pos))

    cos = cos_ref[...]
    sin = sin_ref[...]
    heads = range(hb)
    qs = [_rope128(q_ref[:, hh * RET_DK:(hh + 1) * RET_DK].astype(F32), cos, sin) for hh in heads]
    ks = [_rope128(k_ref[:, hh * RET_DK:(hh + 1) * RET_DK].astype(F32), cos, sin) * (RET_DK ** -0.5)
          for hh in heads]
    vs = [v_ref[:, hh * RET_DV:(hh + 1) * RET_DV] for hh in heads]
    ss = [_nt_dot(qs[hh].astype(BF16), ks[hh].astype(BF16)) for hh in heads]
    sts = [state[hh] for hh in heads]
    cross = [dot((qs[hh] * qdec[hh]).astype(BF16), sts[hh].astype(BF16)) for hh in heads]
    upd = [_tn_dot((ks[hh] * kdec[hh]).astype(BF16), vs[hh]) for hh in heads]
    for hh in heads:
        state[hh] = sts[hh] * jnp.exp(lgs[hh] * float(tile)) + upd[hh]
    for hh in heads:
        o = dot((ss[hh] * dmat[hh]).astype(BF16), vs[hh]) + cross[hh]
        ms = jnp.mean(o * o, axis=-1, keepdims=True)
        y = o * lax.rsqrt(ms + EPS) * gn_ref[...]
        gate = g_ref[:, hh * RET_DV:(hh + 1) * RET_DV].astype(F32)
        o_ref[:, hh * RET_DV:(hh + 1) * RET_DV] = (y * (gate * _sigmoid(gate))).astype(o_ref.dtype)


def _retention(proj, cos, sin, log_gamma, ret_norm, batch, seq):
    tile = min(RET_TILE, seq)
    nt = seq // tile
    hb = RET_HEADS_PER_STEP
    dk, dv = hb * RET_DK, hb * RET_DV
    groups = RET_HEADS // hb
    qb = 0
    kb = groups
    vb = (2 * RET_HEADS * RET_DK) // dv
    gb = vb + groups
    row = lambda b, h, t: b * nt + t
    return pl.pallas_call(
        _retention_kernel,
        grid=(batch, groups, nt),
        in_specs=[
            pl.BlockSpec((tile, dk), lambda b, h, t: (row(b, h, t), qb + h)),
            pl.BlockSpec((tile, dk), lambda b, h, t: (row(b, h, t), kb + h)),
            pl.BlockSpec((tile, dv), lambda b, h, t: (row(b, h, t), vb + h)),
            pl.BlockSpec((tile, dv), lambda b, h, t: (row(b, h, t), gb + h)),
            pl.BlockSpec((tile, LANES), lambda b, h, t: (t, 0)),
            pl.BlockSpec((tile, LANES), lambda b, h, t: (t, 0)),
            pl.BlockSpec((hb, 1, LANES), lambda b, h, t: (h, 0, 0)),
            pl.BlockSpec((1, RET_DV), lambda b, h, t: (0, 0)),
        ],
        out_specs=pl.BlockSpec((tile, dv), lambda b, h, t: (row(b, h, t), h)),
        out_shape=jax.ShapeDtypeStruct((batch * seq, RET_HEADS * RET_DV), BF16),
        scratch_shapes=[
            pltpu.VMEM((hb, RET_DK, RET_DV), F32),
            pltpu.VMEM((hb, tile, tile), F32),
            pltpu.VMEM((hb, tile, RET_DK), F32),
            pltpu.VMEM((hb, tile, RET_DK), F32),
        ],
        compiler_params=_cparams(("parallel", "parallel", "arbitrary")),
        name="retention",
    )(proj, proj, proj, proj, cos, sin, log_gamma, ret_norm.reshape(1, RET_DV))


def _conv_silu(xbuf, x_ref, w_ref):
    tile = x_ref.shape[0]
    xbuf[SUBLANES:SUBLANES + tile, :] = x_ref[...].astype(F32)
    w = w_ref[...]
    base = SUBLANES - (CONV_WIDTH - 1)
    acc = xbuf[base:base + tile, :] * w[0:1, :]
    for j in range(1, CONV_WIDTH):
        acc = acc + xbuf[base + j:base + j + tile, :] * w[j:j + 1, :]
    xbuf[0:SUBLANES, :] = xbuf[tile:tile + SUBLANES, :]
    return acc * _sigmoid(acc)


def _l2norm(x):
    return x * lax.rsqrt(jnp.sum(x * x, axis=-1, keepdims=True) + EPS)


def _softplus(x):
    return jnp.maximum(x, 0.0) + jnp.log1p(jnp.exp(-jnp.abs(x)))


def _unit_lower_inverses(lows):
    c = lows[0].shape[0]
    dot = functools.partial(jnp.dot, preferred_element_type=F32)
    eye = (lax.broadcasted_iota(jnp.int32, (c, c), 0) == lax.broadcasted_iota(jnp.int32, (c, c), 1)).astype(F32)
    invs = [eye - low for low in lows]
    powers = [_split_bf16(low) for low in lows]
    span = 2
    while span < c:
        powers = [_split_bf16(_dot_split(hi, lo, hi, lo)) for hi, lo in powers]
        invs = [inv + _dot_split(*_split_bf16(inv), hi, lo) for inv, (hi, lo) in zip(invs, powers)]
        span *= 2
    return invs


def _split_bf16(a):
    hi = a.astype(BF16)
    return hi, (a - hi.astype(F32)).astype(BF16)


def _dot_split(a_hi, a_lo, b_hi, b_lo):
    dot = functools.partial(jnp.dot, preferred_element_type=F32)
    return dot(a_hi, b_hi) + dot(a_hi, b_lo) + dot(a_lo, b_hi)


def _chunk_cumsum(tril_b, g):
    dot = functools.partial(jnp.dot, preferred_element_type=F32)
    g1 = g.astype(BF16)
    r1 = g - g1.astype(F32)
    g2 = r1.astype(BF16)
    g3 = (r1 - g2.astype(F32)).astype(BF16)
    return dot(tril_b, g1) + dot(tril_b, g2) + dot(tril_b, g3)


def _gdn_kernel(xq_ref, xk_ref, xv_ref, z_ref, aux_ref, wq_ref, wk_ref, wv_ref, alog_ref, dtb_ref,
                gn_ref, o_ref, state, qbuf, kbuf, vbuf):
    hp = pl.program_id(1)
    t = pl.program_id(2)
    tile = xq_ref.shape[0]
    c = GDN_CHUNK
    hb = GDN_HEADS_PER_STEP

    @pl.when(t == 0)
    def _():
        state[...] = jnp.zeros_like(state)
        qbuf[0:SUBLANES, :] = jnp.zeros((SUBLANES, qbuf.shape[1]), F32)
        kbuf[0:SUBLANES, :] = jnp.zeros((SUBLANES, kbuf.shape[1]), F32)
        vbuf[0:SUBLANES, :] = jnp.zeros((SUBLANES, vbuf.shape[1]), F32)

    q_all = _conv_silu(qbuf, xq_ref, wq_ref)
    k_all = _conv_silu(kbuf, xk_ref, wk_ref)
    v_all = _conv_silu(vbuf, xv_ref, wv_ref)

    aux = aux_ref[...]
    lane = lax.broadcasted_iota(jnp.int32, (tile, LANES), 1)
    beta_all = _sigmoid(aux)
    g_all = -jnp.exp(alog_ref[...]) * _softplus(aux + dtb_ref[...])

    ri = lax.broadcasted_iota(jnp.int32, (c, c), 0)
    ci = lax.broadcasted_iota(jnp.int32, (c, c), 1)
    causal = ri >= ci
    strict = ri > ci
    rt = lax.broadcasted_iota(jnp.int32, (tile, tile), 0)
    ct = lax.broadcasted_iota(jnp.int32, (tile, tile), 1)
    same_chunk = lax.shift_right_logical(rt, c.bit_length() - 1) == lax.shift_right_logical(ct, c.bit_length() - 1)
    gcum_all = _chunk_cumsum(((rt >= ct) & same_chunk).astype(BF16), g_all)

    dot = functools.partial(jnp.dot, preferred_element_type=F32)
    nchunk = tile // c
    pairs = []
    for hh in range(hb):
        h = hp * hb + hh
        q = _l2norm(q_all[:, hh * GDN_DK:(hh + 1) * GDN_DK]) * (GDN_DK ** -0.5)
        k = _l2norm(k_all[:, hh * GDN_DK:(hh + 1) * GDN_DK])
        v = v_all[:, hh * GDN_DV:(hh + 1) * GDN_DV]
        beta = jnp.sum(jnp.where(lane == h, beta_all, 0.0), axis=1, keepdims=True)
        gc = jnp.sum(jnp.where(lane == h + GDN_HEADS, gcum_all, 0.0), axis=1, keepdims=True)
        for i in range(nchunk):
            sl = slice(i * c, (i + 1) * c)
            qc, kc, vc, bc = q[sl], k[sl], v[sl], beta[sl]
            gcum = jnp.broadcast_to(gc[sl], (c, c))
            rel = gcum - gcum.T
            decay = jnp.where(causal, jnp.exp(jnp.where(causal, rel, 0.0)), 0.0)
            eg = jnp.exp(gcum)
            kb = kc * bc
            kcb = kc.astype(BF16)
            g_last = gcum[c - 1:c, :]
            pairs.append(dict(
                hh=hh, i=i,
                low=jnp.where(strict, _nt_dot(kb.astype(BF16), kcb) * decay, 0.0),
                attn=jnp.where(causal, _nt_dot(qc.astype(BF16), kcb) * decay, 0.0).astype(BF16),
                vb=(vc * bc).astype(BF16),
                kbe=(kb * eg).astype(BF16),
                qg=(qc * eg).astype(BF16),
                k_tail_t=(kc * jnp.exp(g_last - gcum)).T.astype(BF16),
                sdec=jnp.exp(g_last[:, :1]),
            ))
    pairs.sort(key=lambda p: (p["i"], p["hh"]))
    invs = _unit_lower_inverses([p["low"] for p in pairs])
    for p, inv in zip(pairs, invs):
        inv_b = inv.astype(BF16)
        p["u"] = dot(inv_b, p["vb"])
        p["w"] = dot(inv_b, p["kbe"]).astype(BF16)
    for p in pairs:
        hh, sl = p["hh"], slice(p["i"] * c, (p["i"] + 1) * c)
        st = state[hh]
        stb = st.astype(BF16)
        v_new = (p["u"] - dot(p["w"], stb)).astype(BF16)
        state[hh] = st * p["sdec"] + dot(p["k_tail_t"], v_new)
        o = dot(p["qg"], stb) + dot(p["attn"], v_new)
        ms = jnp.mean(o * o, axis=-1, keepdims=True)
        y = o * lax.rsqrt(ms + EPS) * gn_ref[...]
        z = z_ref[sl, hh * GDN_DV:(hh + 1) * GDN_DV].astype(F32)
        o_ref[sl, hh * GDN_DV:(hh + 1) * GDN_DV] = (y * (z * _sigmoid(z))).astype(o_ref.dtype)


def _gdn(proj, aux, conv_w, alog_row, dtb_row, gdn_norm, batch, seq):
    tile = min(GDN_TILE, seq)
    nt = seq // tile
    hb = GDN_HEADS_PER_STEP
    dk, dv = hb * GDN_DK, hb * GDN_DV
    base = 2 * RET_HEADS * RET_DK + 2 * RET_HEADS * RET_DV
    qb = base // dk
    kb = qb + GDN_HEADS // hb
    vb = (base + 2 * GDN_HEADS * GDN_DK) // dv
    zb = vb + GDN_HEADS // hb
    cvb = (2 * GDN_HEADS * GDN_DK) // dv
    row = lambda b, h, t: b * nt + t
    return pl.pallas_call(
        _gdn_kernel,
        grid=(batch, GDN_HEADS // hb, nt),
        in_specs=[
            pl.BlockSpec((tile, dk), lambda b, h, t: (row(b, h, t), qb + h)),
            pl.BlockSpec((tile, dk), lambda b, h, t: (row(b, h, t), kb + h)),
            pl.BlockSpec((tile, dv), lambda b, h, t: (row(b, h, t), vb + h)),
            pl.BlockSpec((tile, dv), lambda b, h, t: (row(b, h, t), zb + h)),
            pl.BlockSpec((tile, LANES), lambda b, h, t: (row(b, h, t), 0)),
            pl.BlockSpec((CONV_WIDTH, dk), lambda b, h, t: (0, h)),
            pl.BlockSpec((CONV_WIDTH, dk), lambda b, h, t: (0, GDN_HEADS // hb + h)),
            pl.BlockSpec((CONV_WIDTH, dv), lambda b, h, t: (0, cvb + h)),
            pl.BlockSpec((1, LANES), lambda b, h, t: (0, 0)),
            pl.BlockSpec((1, LANES), lambda b, h, t: (0, 0)),
            pl.BlockSpec((1, GDN_DV), lambda b, h, t: (0, 0)),
        ],
        out_specs=pl.BlockSpec((tile, dv), lambda b, h, t: (row(b, h, t), h)),
        out_shape=jax.ShapeDtypeStruct((batch * seq, GDN_HEADS * GDN_DV), BF16),
        scratch_shapes=[
            pltpu.VMEM((hb, GDN_DK, GDN_DV), F32),
            pltpu.VMEM((tile + SUBLANES, dk), F32),
            pltpu.VMEM((tile + SUBLANES, dk), F32),
            pltpu.VMEM((tile + SUBLANES, dv), F32),
        ],
        compiler_params=_cparams(("parallel", "parallel", "arbitrary")),
        name="gated_deltanet",
    )(proj, proj, proj, proj, aux, conv_w, conv_w, conv_w, alog_row, dtb_row, gdn_norm.reshape(1, GDN_DV))


def _rope64(x, cos, sin):
    half = SWA_HEAD_DIM // 2
    lane = lax.broadcasted_iota(jnp.int32, x.shape, x.ndim - 1)
    first = (lane % SWA_HEAD_DIM) < half
    rot = jnp.where(first, pltpu.roll(x, LANES - half, x.ndim - 1), pltpu.roll(x, half, x.ndim - 1))
    return x * cos + rot * sin


def _head_rmsnorm(x, gain, seg_ones):
    sq_hi, sq_lo = _split_bf16(x * x)
    ss = jnp.dot(sq_hi, seg_ones, preferred_element_type=F32) + jnp.dot(sq_lo, seg_ones, preferred_element_type=F32)
    return x * lax.rsqrt(ss * (1.0 / SWA_HEAD_DIM) + EPS) * gain


def _swa_kernel(sink_ref, q_ref, k_ref, v_ref, cos_ref, sin_ref, qn_ref, kn_ref, seg_ref, o_ref,
                kprev, vprev):
    i = pl.program_id(1)
    w = WINDOW
    pairs = SWA_KV_HEADS // 2
    grp = SWA_Q_HEADS // SWA_KV_HEADS
    qchunks = grp * SWA_HEAD_DIM // LANES

    @pl.when(i == 0)
    def _():
        kprev[...] = jnp.zeros_like(kprev)
        vprev[...] = jnp.zeros_like(vprev)

    cos = cos_ref[...]
    sin = sin_ref[...]
    seg = seg_ref[...]
    lane = lax.broadcasted_iota(jnp.int32, (w, LANES), 1)
    lo_mask = lane < SWA_HEAD_DIM

    k_lo, k_hi, v_lo, v_hi = [], [], [], []
    for c in range(pairs):
        kc = k_ref[:, c * LANES:(c + 1) * LANES].astype(F32)
        kc = _rope64(_head_rmsnorm(kc, kn_ref[...], seg), cos, sin)
        vc = v_ref[:, c * LANES:(c + 1) * LANES].astype(F32)
        for src, lo_list, hi_list in ((kc, k_lo, k_hi), (vc, v_lo, v_hi)):
            a_lo = jnp.where(lo_mask, src, 0.0)
            b_hi = jnp.where(lo_mask, 0.0, src)
            lo_list += [a_lo, pltpu.roll(b_hi, SWA_HEAD_DIM, 1)]
            hi_list += [pltpu.roll(a_lo, SWA_HEAD_DIM, 1), b_hi]

    qpos = lax.broadcasted_iota(jnp.int32, (w, 2 * w), 0) + w
    kpos = lax.broadcasted_iota(jnp.int32, (w, 2 * w), 1)
    rel = qpos - kpos
    first_key = jnp.where(i > 0, 0, w)
    valid = (rel >= 0) & (rel < w) & (kpos >= first_key)
    scale = SWA_HEAD_DIM ** -0.5 * LOG2E
    cos_q = jnp.concatenate([cos] * qchunks, axis=0)
    sin_q = jnp.concatenate([sin] * qchunks, axis=0)

    for kvh in range(SWA_KV_HEADS):
        q0 = kvh * grp * SWA_HEAD_DIM
        q2 = jnp.concatenate(
            [q_ref[:, q0 + c * LANES:q0 + (c + 1) * LANES].astype(F32) for c in range(qchunks)], axis=0)
        q2 = _head_rmsnorm(q2, qn_ref[...], seg)
        q2 = (_rope64(q2, cos_q, sin_q) * scale).astype(BF16)
        kl = jnp.concatenate([kprev[0, kvh], k_lo[kvh].astype(BF16)], axis=0)
        kh = jnp.concatenate([kprev[1, kvh], k_hi[kvh].astype(BF16)], axis=0)
        vl = jnp.concatenate([vprev[0, kvh], v_lo[kvh].astype(BF16)], axis=0)
        vh = jnp.concatenate([vprev[1, kvh], v_hi[kvh].astype(BF16)], axis=0)
        acc = None
        for par, kk, vv in ((0, kl, vl), (1, kh, vh)):
            sc = _nt_dot(q2, kk)
            probs, denoms = [], []
            for c in range(qchunks):
                sink = sink_ref[kvh * grp + 2 * c + par] * LOG2E
                s = jnp.where(valid, sc[c * w:(c + 1) * w, :], -jnp.inf)
                mx = jnp.maximum(jnp.max(s, axis=-1, keepdims=True), sink)
                p = jnp.exp2(s - mx)
                denoms.append(jnp.sum(p, axis=-1, keepdims=True) + jnp.exp2(sink - mx))
                probs.append(p.astype(BF16))
            pv = jnp.dot(jnp.concatenate(probs, axis=0), vv, preferred_element_type=F32)
            pv = [pv[c * w:(c + 1) * w, :] * (1.0 / denoms[c]) for c in range(qchunks)]
            acc = pv if acc is None else [a + b for a, b in zip(acc, pv)]
        for c in range(qchunks):
            o_ref[:, q0 + c * LANES:q0 + (c + 1) * LANES] = acc[c].astype(o_ref.dtype)

    for kvh in range(SWA_KV_HEADS):
        kprev[0, kvh] = k_lo[kvh].astype(BF16)
        kprev[1, kvh] = k_hi[kvh].astype(BF16)
        vprev[0, kvh] = v_lo[kvh].astype(BF16)
        vprev[1, kvh] = v_hi[kvh].astype(BF16)


def _swa(proj, cos, sin, q_norm, k_norm, sinks, batch, seq):
    w = WINDOW
    nb = seq // w
    qw = SWA_Q_HEADS * SWA_HEAD_DIM
    kvw = SWA_KV_HEADS * SWA_HEAD_DIM
    kblk = qw // kvw
    reps = LANES // SWA_HEAD_DIM
    seg = (jnp.arange(LANES)[:, None] // SWA_HEAD_DIM == jnp.arange(LANES)[None, :] // SWA_HEAD_DIM).astype(BF16)
    return pl.pallas_call(
        _swa_kernel,
        grid=(batch, nb),
        in_specs=[
            pl.BlockSpec(memory_space=pltpu.SMEM),
            pl.BlockSpec((w, qw), lambda b, i: (b * nb + i, 0)),
            pl.BlockSpec((w, kvw), lambda b, i: (b * nb + i, kblk)),
            pl.BlockSpec((w, kvw), lambda b, i: (b * nb + i, kblk + 1)),
            pl.BlockSpec((w, LANES), lambda b, i: (i, 0)),
            pl.BlockSpec((w, LANES), lambda b, i: (i, 0)),
            pl.BlockSpec((1, LANES), lambda b, i: (0, 0)),
            pl.BlockSpec((1, LANES), lambda b, i: (0, 0)),
            pl.BlockSpec((LANES, LANES), lambda b, i: (0, 0)),
        ],
        out_specs=pl.BlockSpec((w, qw), lambda b, i: (b * nb + i, 0)),
        out_shape=jax.ShapeDtypeStruct((batch * seq, qw), BF16),
        scratch_shapes=[
            pltpu.VMEM((2, SWA_KV_HEADS, w, LANES), BF16),
            pltpu.VMEM((2, SWA_KV_HEADS, w, LANES), BF16),
        ],
        compiler_params=_cparams(("parallel", "arbitrary")),
        name="swa",
    )(sinks.astype(F32), proj, proj, proj, cos, sin,
      jnp.tile(q_norm.astype(F32), reps).reshape(1, LANES),
      jnp.tile(k_norm.astype(F32), reps).reshape(1, LANES), seg)


def _route(logits):
    tm = logits.shape[0]
    lane = lax.broadcasted_iota(jnp.int32, (tm, LANES), 1)
    lane_f = lane.astype(F32)
    big = float(LANES)
    neg = -jnp.inf
    gl = jnp.where(lane < N_GROUPS, logits, neg)
    gmax = jnp.max(gl, axis=1, keepdims=True)
    gidx = jnp.min(jnp.where(gl == gmax, lane_f, big), axis=1, keepdims=True)
    g_p = 1.0 / jnp.sum(jnp.exp(gl - gmax), axis=1, keepdims=True)
    egroup = lax.shift_right_arithmetic(lane - N_GROUPS, 3).astype(F32)
    in_group = (lane >= N_GROUPS) & (lane < N_GROUPS + N_EXPERTS) & (egroup == gidx)
    el = jnp.where(in_group, logits, neg)
    emax = jnp.max(el, axis=1, keepdims=True)
    ee = jnp.exp(el - emax)
    prob = ee / jnp.sum(ee, axis=1, keepdims=True)
    pm = jnp.where(in_group, prob, -1.0)
    p1 = jnp.max(pm, axis=1, keepdims=True)
    i1 = jnp.min(jnp.where(pm == p1, lane_f, big), axis=1, keepdims=True)
    pm2 = jnp.where(lane_f == i1, -1.0, pm)
    p2 = jnp.max(pm2, axis=1, keepdims=True)
    i2 = jnp.min(jnp.where(pm2 == p2, lane_f, big), axis=1, keepdims=True)
    den = p1 + p2
    out = jnp.where(lane == 0, i1 - N_GROUPS, 0.0)
    out = jnp.where(lane == 1, i2 - N_GROUPS, out)
    out = jnp.where(lane == 2, g_p * p1 / den, out)
    out = jnp.where(lane == 3, g_p * p2 / den, out)
    return out


def _out_proj_kernel(*refs, n_act):
    acts = refs[:n_act]
    ws = refs[n_act:2 * n_act]
    x_ref, g_ref, wr_ref, x1_ref, h_ref, route_ref, y_even, y_odd = refs[2 * n_act:]
    i = pl.program_id(0)
    tm = x_ref.shape[0]

    def step(prev, cur):
        y = prev[...]
        x1_ref[...] = y
        ms = jnp.mean(y * y, axis=-1, keepdims=True)
        h = y * lax.rsqrt(ms + EPS) * g_ref[...]
        for s in range(ROW_TILES):
            h_ref[pl.ds(s, tm, stride=ROW_TILES), :] = h[:, s * LANES:(s + 1) * LANES]
        h_hi, h_lo = _split_bf16(h)
        wr = wr_ref[...]
        lg2 = jnp.dot(h_hi, wr, preferred_element_type=F32) + jnp.dot(h_lo, wr, preferred_element_type=F32)
        route_ref[...] = _route(lg2[:, :LANES] + lg2[:, LANES:])
        acc = x_ref[...]
        for a, w in zip(acts, ws):
            acc = acc + jnp.dot(a[...], w[...], preferred_element_type=F32)
        cur[...] = acc

    @pl.when(i == 0)
    def _():
        y_odd[...] = jnp.zeros_like(y_odd)

    @pl.when(i % 2 == 0)
    def _():
        step(y_odd, y_even)

    @pl.when(i % 2 == 1)
    def _():
        step(y_even, y_odd)


def _out_proj(acts, ws, x, gain, w_router, *, tm=256):
    n, d = x.shape
    tm = min(tm, n)
    n_act = len(acts)
    nt = n // tm
    const = lambda i: (0, 0)
    cur = lambda i: (jnp.minimum(i, nt - 1), 0)
    prev = lambda i: (jnp.maximum(i - 1, 0), 0)
    in_specs = [pl.BlockSpec((tm, a.shape[1]), cur) for a in acts]
    in_specs += [pl.BlockSpec(w.shape, const, pipeline_mode=pl.Buffered(1)) for w in ws]
    in_specs += [
        pl.BlockSpec((tm, d), cur),
        pl.BlockSpec((1, d), const),
        pl.BlockSpec((d, 2 * LANES), const),
    ]
    return pl.pallas_call(
        functools.partial(_out_proj_kernel, n_act=n_act),
        grid=(nt + 1,),
        in_specs=in_specs,
        out_specs=[
            pl.BlockSpec((tm, d), prev),
            pl.BlockSpec((tm * ROW_TILES, LANES), prev),
            pl.BlockSpec((tm, LANES), prev),
        ],
        out_shape=[
            jax.ShapeDtypeStruct((n, d), F32),
            jax.ShapeDtypeStruct((n * ROW_TILES, LANES), F32),
            jax.ShapeDtypeStruct((n, LANES), F32),
        ],
        scratch_shapes=[pltpu.VMEM((tm, d), F32), pltpu.VMEM((tm, d), F32)],
        compiler_params=_cparams(("arbitrary",)),
        name="out_proj_router",
    )(*acts, *ws, x, gain.reshape(1, d), w_router)


def _row_gather_start(idx_ref, base, count, src_hbm, dst, sem, *, both_queues):
    group = 8

    def body(gi, carry):
        for j in range(group):
            r = gi * group + j
            tok = idx_ref[base + r]
            pltpu.make_async_copy(src_hbm.at[pl.ds(pl.multiple_of(tok * ROW_TILES, ROW_TILES), ROW_TILES)],
                                  dst.at[:, pl.ds(r, 1), :],
                                  sem).start(priority=j % 2 if both_queues else 0)
        return carry
    lax.fori_loop(0, count // group, body, 0)


def _row_gather_wait(dst, sem):
    pltpu.make_async_copy(dst, dst, sem).wait()


def _expert_kernel(be_ref, nu_ref, nxt_ref, tok_ref, h_hbm, wg_hbm, wu_hbm, wd_hbm, o_ref,
                   xbuf_even, xbuf_odd, stage_g, stage_u, stage_d, wg_b, wu_b, wd_b, gsem, wsem, *, layer):
    i = pl.program_id(0)
    bm = MOE_BM
    nu = nu_ref[0]
    last_block = pl.num_programs(0) - 1
    e = be_ref[i]
    weight_queue = 1

    def weight_copies(expert):
        return (pltpu.make_async_copy(wg_hbm.at[layer, expert], stage_g, wsem.at[0]),
                pltpu.make_async_copy(wu_hbm.at[layer, expert], stage_u, wsem.at[1]),
                pltpu.make_async_copy(wd_hbm.at[layer, expert], stage_d, wsem.at[2]))

    @pl.when(i == 0)
    def _():
        _row_gather_start(tok_ref, 0, bm, h_hbm, xbuf_even, gsem.at[0], both_queues=False)
        for cp in weight_copies(e):
            cp.start(priority=weight_queue)

    def run_block(cur, cur_sem, nxt_buf, nxt_sem):
        first_of_run = jnp.logical_or(i == 0, e != be_ref[jnp.maximum(i - 1, 0)])

        @pl.when(first_of_run)
        def _():
            for cp, stage, dst in zip(weight_copies(e), (stage_g, stage_u, stage_d), (wg_b, wu_b, wd_b)):
                cp.wait()
                dst[...] = stage[...].astype(BF16)
            nxt = nxt_ref[e]

            @pl.when(nxt >= 0)
            def _():
                for cp in weight_copies(nxt):
                    cp.start(priority=weight_queue)

        _row_gather_wait(cur, cur_sem)
        base = jnp.minimum(i + 1, last_block) * bm
        for r in range(bm):
            tok = tok_ref[base + r]
            pltpu.make_async_copy(h_hbm.at[pl.ds(pl.multiple_of(tok * ROW_TILES, ROW_TILES), ROW_TILES)],
                                  nxt_buf.at[:, pl.ds(r, 1), :], nxt_sem).start()
        x = jnp.concatenate([cur[s].astype(BF16) for s in range(ROW_TILES)], axis=1)
        g = jnp.dot(x, wg_b[...], preferred_element_type=F32)
        u = jnp.dot(x, wu_b[...], preferred_element_type=F32)
        hid = (g * _sigmoid(g) * u).astype(BF16)
        y = jnp.dot(hid, wd_b[...], preferred_element_type=F32)
        for s in range(ROW_TILES):
            o_ref[pl.ds(s, bm, stride=ROW_TILES), :] = y[:, s * LANES:(s + 1) * LANES]

        @pl.when(i + 1 >= nu)
        def _():
            _row_gather_wait(nxt_buf, nxt_sem)

    @pl.when(jnp.logical_and(i < nu, i % 2 == 0))
    def _():
        run_block(xbuf_even, gsem.at[0], xbuf_odd, gsem.at[1])

    @pl.when(jnp.logical_and(i < nu, i % 2 == 1))
    def _():
        run_block(xbuf_odd, gsem.at[1], xbuf_even, gsem.at[0])

    @pl.when(i >= nu)
    def _():
        o_ref[...] = jnp.zeros_like(o_ref)


def _expert_mlp(h_tiled, tok_pad, block_expert, n_used, next_expert, w_gate, w_up, w_down, layer):
    n_pad = tok_pad.shape[0]
    bm = MOE_BM
    _, _, d, ff = w_gate.shape
    any_spec = pl.BlockSpec(memory_space=pl.ANY)
    grid_spec = pltpu.PrefetchScalarGridSpec(
        num_scalar_prefetch=4,
        grid=(n_pad // bm,),
        in_specs=[any_spec, any_spec, any_spec, any_spec],
        out_specs=pl.BlockSpec((bm * ROW_TILES, LANES), lambda i, *_: (i, 0)),
        scratch_shapes=[
            pltpu.VMEM((ROW_TILES, bm, LANES), F32), pltpu.VMEM((ROW_TILES, bm, LANES), F32),
            pltpu.VMEM((d, ff), F32), pltpu.VMEM((d, ff), F32), pltpu.VMEM((ff, d), F32),
            pltpu.VMEM((d, ff), BF16), pltpu.VMEM((d, ff), BF16), pltpu.VMEM((ff, d), BF16),
            pltpu.SemaphoreType.DMA((2,)),
            pltpu.SemaphoreType.DMA((3,)),
        ],
    )
    return pl.pallas_call(
        functools.partial(_expert_kernel, layer=layer),
        grid_spec=grid_spec,
        out_shape=jax.ShapeDtypeStruct((n_pad * ROW_TILES, LANES), F32),
        compiler_params=_cparams(("arbitrary",)),
        name="expert_mlp",
    )(block_expert, n_used, next_expert, tok_pad, h_tiled.reshape(-1, 1, LANES), w_gate, w_up, w_down)


def _combine_kernel(dest_ref, x1_ref, route_ref, y_hbm, o_ref, ybuf, sem):
    i = pl.program_id(0)
    nsteps = pl.num_programs(0)
    tm = x1_ref.shape[0]
    slot = i % 2

    def start(step, sl):
        for k in range(2):
            _row_gather_start(dest_ref, k * (nsteps * tm) + step * tm, tm, y_hbm, ybuf.at[sl, k], sem.at[sl],
                              both_queues=True)

    @pl.when(i == 0)
    def _():
        start(0, 0)

    @pl.when(i + 1 < nsteps)
    def _():
        start(i + 1, 1 - slot)

    _row_gather_wait(ybuf.at[slot], sem.at[slot])
    route = route_ref[...]
    g0 = jnp.broadcast_to(route[:, 2:3], (tm, LANES))
    g1 = jnp.broadcast_to(route[:, 3:4], (tm, LANES))
    for s in range(ROW_TILES):
        o_ref[:, s * LANES:(s + 1) * LANES] = (x1_ref[:, s * LANES:(s + 1) * LANES]
                                               + g0 * ybuf[slot, 0, s] + g1 * ybuf[slot, 1, s])


def _combine(x1, route, dest, y_tiled, *, tm=256):
    n, d = x1.shape
    tm = min(tm, n)
    grid_spec = pltpu.PrefetchScalarGridSpec(
        num_scalar_prefetch=1,
        grid=(n // tm,),
        in_specs=[
            pl.BlockSpec((tm, d), lambda i, *_: (i, 0)),
            pl.BlockSpec((tm, LANES), lambda i, *_: (i, 0)),
            pl.BlockSpec(memory_space=pl.ANY),
        ],
        out_specs=pl.BlockSpec((tm, d), lambda i, *_: (i, 0)),
        scratch_shapes=[
            pltpu.VMEM((2, 2, ROW_TILES, tm, LANES), F32),
            pltpu.SemaphoreType.DMA((2,)),
        ],
    )
    return pl.pallas_call(
        _combine_kernel,
        grid_spec=grid_spec,
        out_shape=jax.ShapeDtypeStruct((n, d), F32),
        compiler_params=_cparams(("arbitrary",)),
        name="moe_combine",
    )(dest, x1, route, y_tiled.reshape(-1, 1, LANES))


def _moe(x1, h_tiled, route, w_gate, w_up, w_down, layer):
    n, d = x1.shape
    bm = MOE_BM
    n_assign = 2 * n
    expert_id = route[:, 0:2].astype(jnp.int32).reshape(-1)
    onehot = (expert_id[:, None] == jnp.arange(N_EXPERTS, dtype=jnp.int32)[None, :]).astype(jnp.int32)
    csum = jnp.cumsum(onehot, axis=0)
    rank = jnp.sum(csum * onehot, axis=1) - 1
    counts = csum[-1]
    padded = (counts + bm - 1) // bm * bm
    pend = jnp.cumsum(padded)
    pstart = pend - padded
    dest = (jnp.sum(onehot * pstart[None, :], axis=1) + rank).astype(jnp.int32)
    n_pad = (n_assign + N_EXPERTS * (bm - 1) + bm - 1) // bm * bm
    n_blk = n_pad // bm
    token_id = jnp.arange(n_assign, dtype=jnp.int32) // 2
    tok_pad = (jnp.arange(n_pad, dtype=jnp.int32) % n).at[dest].set(token_id)
    block_start = jnp.arange(n_blk, dtype=jnp.int32) * bm
    block_expert = jnp.minimum(jnp.sum(pend[None, :] <= block_start[:, None], axis=1), N_EXPERTS - 1)
    n_used = (pend[-1] // bm).astype(jnp.int32).reshape(1)
    ids = jnp.arange(N_EXPERTS, dtype=jnp.int32)
    cand = jnp.where((counts[None, :] > 0) & (ids[None, :] > ids[:, None]), ids[None, :], N_EXPERTS)
    next_expert = jnp.min(cand, axis=1)
    next_expert = jnp.where(next_expert >= N_EXPERTS, -1, next_expert).astype(jnp.int32)
    y_tiled = _expert_mlp(h_tiled, tok_pad, block_expert.astype(jnp.int32), n_used, next_expert,
                          w_gate, w_up, w_down, layer)
    dest_by_choice = dest.reshape(n, 2).T.reshape(-1).astype(jnp.int32)
    return _combine(x1, route, dest_by_choice, y_tiled)


def _router_weights(w_group, w_expert):
    d = w_group.shape[0]
    pad = jnp.zeros((d, LANES - N_GROUPS - N_EXPERTS), F32)
    w = jnp.concatenate([w_group.astype(F32), w_expert.astype(F32), pad], axis=1)
    w_hi = w.astype(BF16)
    w_lo = (w - w_hi.astype(F32)).astype(BF16)
    return jnp.concatenate([w_hi, w_lo], axis=1)


def kernel(x, norm_mix, norm_ffn, even_w_in, ret_norm, gdn_conv, gdn_a_log, gdn_dt_bias, gdn_norm,
           even_w_out, odd_w_in, q_norm, k_norm, attn_sinks, odd_w_out, router_group, router_expert,
           expert_w_gate, expert_w_up, expert_w_down):
    batch, seq, d = x.shape
    n = batch * seq
    xt = x.reshape(n, d)

    w_in = even_w_in[0]
    w_main = w_in[:, :EVEN_MAIN].astype(BF16)
    w_aux = jnp.pad(w_in[:, EVEN_MAIN:], ((0, 0), (0, LANES - 2 * GDN_HEADS))).astype(BF16)
    proj, aux = _norm_proj(xt, norm_mix[0], w_main, w_aux, tn=EVEN_MAIN // 8)
    cos_r, sin_r = _rope_tables(seq, RET_DK // 2)
    log_gamma = jnp.log1p(-jnp.exp2(-5.0 - jnp.arange(RET_HEADS, dtype=F32)))
    log_gamma = jnp.broadcast_to(log_gamma[:, None, None], (RET_HEADS, 1, LANES))
    o_ret = _retention(proj, cos_r, sin_r, log_gamma, ret_norm[0].astype(F32), batch, seq)
    lane_pad = (GDN_HEADS, LANES - 2 * GDN_HEADS)
    alog_row = jnp.pad(gdn_a_log[0].astype(F32), lane_pad).reshape(1, LANES)
    dtb_row = jnp.pad(gdn_dt_bias[0].astype(F32), lane_pad).reshape(1, LANES)
    o_gdn = _gdn(proj, aux, gdn_conv[0].astype(F32), alog_row, dtb_row, gdn_norm[0].astype(F32), batch, seq)
    w_out = even_w_out[0].astype(BF16)
    split = RET_HEADS * RET_DV
    x1, h2, route = _out_proj([o_ret, o_gdn], [w_out[:split], w_out[split:]], xt, norm_ffn[0],
                              _router_weights(router_group[0], router_expert[0]))
    xt = _moe(x1, h2, route, expert_w_gate, expert_w_up, expert_w_down, 0)

    proj = _norm_proj(xt, norm_mix[1], odd_w_in[0].astype(BF16), tn=ODD_IN // 2)
    cos_s, sin_s = _rope_tables(seq, SWA_HEAD_DIM // 2)
    o_swa = _swa(proj, cos_s, sin_s, q_norm[0], k_norm[0], attn_sinks[0], batch, seq)
    x1, h2, route = _out_proj([o_swa], [odd_w_out[0].astype(BF16)], xt, norm_ffn[1],
                              _router_weights(router_group[1], router_expert[1]))
    xt = _moe(x1, h2, route, expert_w_gate, expert_w_up, expert_w_down, 1)
    return xt.reshape(batch, seq, d)
```

```python
import functools

import jax
import jax.numpy as jnp
from jax import lax
from jax.experimental import pallas as pl
from jax.experimental.pallas import tpu as pltpu

F32 = jnp.float32
BF16 = jnp.bfloat16

D_MODEL = 2048
RET_HEADS = 8
RET_DK = 128
RET_DV = 256
GDN_HEADS = 8
GDN_DK = 128
GDN_DV = 256
CONV_WIDTH = 4
SWA_Q_HEADS = 32
SWA_KV_HEADS = 4
SWA_HEAD_DIM = 64
WINDOW = 128
ROPE_THETA = 10000.0
N_GROUPS = 4
EXPERTS_PER_GROUP = 8
N_EXPERTS = N_GROUPS * EXPERTS_PER_GROUP
D_FF_EXPERT = 768
EPS = 1e-6
LOG2E = 1.4426950408889634

LANES = 128
SUBLANES = 8
VMEM_LIMIT = 56 * 1024 * 1024

EVEN_MAIN = 2 * RET_HEADS * RET_DK + 2 * RET_HEADS * RET_DV + 2 * GDN_HEADS * GDN_DK + 2 * GDN_HEADS * GDN_DV
ODD_IN = (SWA_Q_HEADS + 2 * SWA_KV_HEADS) * SWA_HEAD_DIM

RET_TILE = 256
RET_HEADS_PER_STEP = 4
GDN_TILE = 256
GDN_CHUNK = 128
GDN_HEADS_PER_STEP = 8
MOE_BM = 256
ROW_TILES = D_MODEL // LANES


def _cparams(sem):
    return pltpu.CompilerParams(dimension_semantics=sem, vmem_limit_bytes=VMEM_LIMIT)


def _sigmoid(x):
    return 1.0 / (1.0 + jnp.exp(-x))


def _nt_dot(a, b):
    return lax.dot_general(a, b, (((1,), (1,)), ((), ())), preferred_element_type=F32)


def _tn_dot(a, b):
    return lax.dot_general(a, b, (((0,), (0,)), ((), ())), preferred_element_type=F32)


def _rope_table_kernel(cos_ref, sin_ref, *, half):
    rows = cos_ref.shape[0]
    r0 = pl.program_id(0) * rows
    pos = (lax.broadcasted_iota(jnp.int32, (rows, LANES), 0) + r0).astype(F32)
    lane = lax.broadcasted_iota(jnp.int32, (rows, LANES), 1)
    fi = (lane % half).astype(F32)
    inv_freq = jnp.exp(-(fi / half) * jnp.log(ROPE_THETA))
    ang = pos * inv_freq
    first = (lane % (2 * half)) < half
    cos_ref[...] = jnp.cos(ang)
    sin_ref[...] = jnp.where(first, -jnp.sin(ang), jnp.sin(ang))


def _rope_tables(seq, half):
    rows = min(seq, 1024)
    return pl.pallas_call(
        functools.partial(_rope_table_kernel, half=half),
        grid=(seq // rows,),
        out_specs=[pl.BlockSpec((rows, LANES), lambda i: (i, 0))] * 2,
        out_shape=[jax.ShapeDtypeStruct((seq, LANES), F32)] * 2,
        compiler_params=_cparams(("arbitrary",)),
        name="rope_tables",
    )()


def _norm_proj_kernel(x_ref, g_ref, w_ref, *rest, with_aux):
    if with_aux:
        waux_ref, o_ref, aux_ref, h_scr = rest
    else:
        o_ref, h_scr = rest
    j = pl.program_id(1)

    @pl.when(j == 0)
    def _():
        x = x_ref[...]
        ms = jnp.mean(x * x, axis=-1, keepdims=True)
        h = x * lax.rsqrt(ms + EPS) * g_ref[...]
        h_scr[...] = h.astype(BF16)
        if with_aux:
            aux_ref[...] = jnp.dot(h_scr[...], waux_ref[...], preferred_element_type=F32)

    o_ref[...] = jnp.dot(h_scr[...], w_ref[...], preferred_element_type=F32).astype(o_ref.dtype)


def _norm_proj(x, gain, w, w_aux=None, *, tm=1024, tn=1024):
    n, d = x.shape
    n_out = w.shape[1]
    tm = min(tm, n)
    tn = min(tn, n_out)
    with_aux = w_aux is not None
    in_specs = [
        pl.BlockSpec((tm, d), lambda i, j: (i, 0)),
        pl.BlockSpec((1, d), lambda i, j: (0, 0)),
        pl.BlockSpec((d, tn), lambda i, j: (0, j)),
    ]
    out_specs = [pl.BlockSpec((tm, tn), lambda i, j: (i, j))]
    out_shape = [jax.ShapeDtypeStruct((n, n_out), BF16)]
    args = [x, gain.reshape(1, d), w]
    if with_aux:
        in_specs.append(pl.BlockSpec((d, LANES), lambda i, j: (0, 0)))
        out_specs.append(pl.BlockSpec((tm, LANES), lambda i, j: (i, 0)))
        out_shape.append(jax.ShapeDtypeStruct((n, LANES), F32))
        args.append(w_aux)
    res = pl.pallas_call(
        functools.partial(_norm_proj_kernel, with_aux=with_aux),
        grid=(n // tm, n_out // tn),
        in_specs=in_specs,
        out_specs=out_specs,
        out_shape=out_shape,
        scratch_shapes=[pltpu.VMEM((tm, d), BF16)],
        compiler_params=_cparams(("parallel", "arbitrary")),
        name="norm_proj",
    )(*args)
    return res if with_aux else res[0]


def _rope128(x, cos, sin):
    return x * cos + pltpu.roll(x, RET_DK // 2, 1) * sin


def _retention_kernel(q_ref, k_ref, v_ref, g_ref, cos_ref, sin_ref, lg_ref, gn_ref, o_ref,
                      state, dmat, qdec, kdec):
    t = pl.program_id(2)
    tile = q_ref.shape[0]
    hb = RET_HEADS_PER_STEP
    dot = functools.partial(jnp.dot, preferred_element_type=F32)
    lgs = [lg_ref[hh][:, :1] for hh in range(hb)]

    @pl.when(t == 0)
    def _():
        state[...] = jnp.zeros_like(state)
        ri = lax.broadcasted_iota(jnp.int32, (tile, tile), 0)
        ci = lax.broadcasted_iota(jnp.int32, (tile, tile), 1)
        causal = ri >= ci
        rel = jnp.where(causal, ri - ci, 0).astype(F32)
        pos = lax.broadcasted_iota(jnp.int32, (tile, RET_DK), 0).astype(F32)
        for hh, lg in enumerate(lgs):
            dmat[hh] = jnp.where(causal, jnp.exp(lg * rel), 0.0)
            qdec[hh] = jnp.exp(lg * (pos + 1.0))
            kdec[hh] = jnp.exp(lg * (tile - 1.0 - pos))

    cos = cos_ref[...]
    sin = sin_ref[...]
    heads = range(hb)
    qs = [_rope128(q_ref[:, hh * RET_DK:(hh + 1) * RET_DK].astype(F32), cos, sin) for hh in heads]
    ks = [_rope128(k_ref[:, hh * RET_DK:(hh + 1) * RET_DK].astype(F32), cos, sin) * (RET_DK ** -0.5)
          for hh in heads]
    vs = [v_ref[:, hh * RET_DV:(hh + 1) * RET_DV] for hh in heads]
    ss = [_nt_dot(qs[hh].astype(BF16), ks[hh].astype(BF16)) for hh in heads]
    sts = [state[hh] for hh in heads]
    cross = [dot((qs[hh] * qdec[hh]).astype(BF16), sts[hh].astype(BF16)) for hh in heads]
    upd = [_tn_dot((ks[hh] * kdec[hh]).astype(BF16), vs[hh]) for hh in heads]
    for hh in heads:
        state[hh] = sts[hh] * jnp.exp(lgs[hh] * float(tile)) + upd[hh]
    for hh in heads:
        o = dot((ss[hh] * dmat[hh]).astype(BF16), vs[hh]) + cross[hh]
        ms = jnp.mean(o * o, axis=-1, keepdims=True)
        y = o * lax.rsqrt(ms + EPS) * gn_ref[...]
        gate = g_ref[:, hh * RET_DV:(hh + 1) * RET_DV].astype(F32)
        o_ref[:, hh * RET_DV:(hh + 1) * RET_DV] = (y * (gate * _sigmoid(gate))).astype(o_ref.dtype)


def _retention(proj, cos, sin, log_gamma, ret_norm, batch, seq):
    tile = min(RET_TILE, seq)
    nt = seq // tile
    hb = RET_HEADS_PER_STEP
    dk, dv = hb * RET_DK, hb * RET_DV
    groups = RET_HEADS // hb
    qb = 0
    kb = groups
    vb = (2 * RET_HEADS * RET_DK) // dv
    gb = vb + groups
    row = lambda b, h, t: b * nt + t
    return pl.pallas_call(
        _retention_kernel,
        grid=(batch, groups, nt),
        in_specs=[
            pl.BlockSpec((tile, dk), lambda b, h, t: (row(b, h, t), qb + h)),
            pl.BlockSpec((tile, dk), lambda b, h, t: (row(b, h, t), kb + h)),
            pl.BlockSpec((tile, dv), lambda b, h, t: (row(b, h, t), vb + h)),
            pl.BlockSpec((tile, dv), lambda b, h, t: (row(b, h, t), gb + h)),
            pl.BlockSpec((tile, LANES), lambda b, h, t: (t, 0)),
            pl.BlockSpec((tile, LANES), lambda b, h, t: (t, 0)),
            pl.BlockSpec((hb, 1, LANES), lambda b, h, t: (h, 0, 0)),
            pl.BlockSpec((1, RET_DV), lambda b, h, t: (0, 0)),
        ],
        out_specs=pl.BlockSpec((tile, dv), lambda b, h, t: (row(b, h, t), h)),
        out_shape=jax.ShapeDtypeStruct((batch * seq, RET_HEADS * RET_DV), BF16),
        scratch_shapes=[
            pltpu.VMEM((hb, RET_DK, RET_DV), F32),
            pltpu.VMEM((hb, tile, tile), F32),
            pltpu.VMEM((hb, tile, RET_DK), F32),
            pltpu.VMEM((hb, tile, RET_DK), F32),
        ],
        compiler_params=_cparams(("parallel", "parallel", "arbitrary")),
        name="retention",
    )(proj, proj, proj, proj, cos, sin, log_gamma, ret_norm.reshape(1, RET_DV))


def _conv_silu(xbuf, x_ref, w_ref):
    tile = x_ref.shape[0]
    xbuf[SUBLANES:SUBLANES + tile, :] = x_ref[...].astype(F32)
    w = w_ref[...]
    base = SUBLANES - (CONV_WIDTH - 1)
    acc = xbuf[base:base + tile, :] * w[0:1, :]
    for j in range(1, CONV_WIDTH):
        acc = acc + xbuf[base + j:base + j + tile, :] * w[j:j + 1, :]
    xbuf[0:SUBLANES, :] = xbuf[tile:tile + SUBLANES, :]
    return acc * _sigmoid(acc)


def _l2norm(x):
    return x * lax.rsqrt(jnp.sum(x * x, axis=-1, keepdims=True) + EPS)


def _softplus(x):
    return jnp.maximum(x, 0.0) + jnp.log1p(jnp.exp(-jnp.abs(x)))


def _unit_lower_inverses(lows):
    c = lows[0].shape[0]
    dot = functools.partial(jnp.dot, preferred_element_type=F32)
    eye = (lax.broadcasted_iota(jnp.int32, (c, c), 0) == lax.broadcasted_iota(jnp.int32, (c, c), 1)).astype(F32)
    invs = [eye - low for low in lows]
    powers = [_split_bf16(low) for low in lows]
    span = 2
    while span < c:
        powers = [_split_bf16(_dot_split(hi, lo, hi, lo)) for hi, lo in powers]
        invs = [inv + _dot_split(*_split_bf16(inv), hi, lo) for inv, (hi, lo) in zip(invs, powers)]
        span *= 2
    return invs


def _split_bf16(a):
    hi = a.astype(BF16)
    return hi, (a - hi.astype(F32)).astype(BF16)


def _dot_split(a_hi, a_lo, b_hi, b_lo):
    dot = functools.partial(jnp.dot, preferred_element_type=F32)
    return dot(a_hi, b_hi) + dot(a_hi, b_lo) + dot(a_lo, b_hi)


def _chunk_cumsum(tril_b, g):
    dot = functools.partial(jnp.dot, preferred_element_type=F32)
    g1 = g.astype(BF16)
    r1 = g - g1.astype(F32)
    g2 = r1.astype(BF16)
    g3 = (r1 - g2.astype(F32)).astype(BF16)
    return dot(tril_b, g1) + dot(tril_b, g2) + dot(tril_b, g3)


def _gdn_kernel(xq_ref, xk_ref, xv_ref, z_ref, aux_ref, wq_ref, wk_ref, wv_ref, alog_ref, dtb_ref,
                gn_ref, o_ref, state, qbuf, kbuf, vbuf):
    hp = pl.program_id(1)
    t = pl.program_id(2)
    tile = xq_ref.shape[0]
    c = GDN_CHUNK
    hb = GDN_HEADS_PER_STEP

    @pl.when(t == 0)
    def _():
        state[...] = jnp.zeros_like(state)
        qbuf[0:SUBLANES, :] = jnp.zeros((SUBLANES, qbuf.shape[1]), F32)
        kbuf[0:SUBLANES, :] = jnp.zeros((SUBLANES, kbuf.shape[1]), F32)
        vbuf[0:SUBLANES, :] = jnp.zeros((SUBLANES, vbuf.shape[1]), F32)

    q_all = _conv_silu(qbuf, xq_ref, wq_ref)
    k_all = _conv_silu(kbuf, xk_ref, wk_ref)
    v_all = _conv_silu(vbuf, xv_ref, wv_ref)

    aux = aux_ref[...]
    lane = lax.broadcasted_iota(jnp.int32, (tile, LANES), 1)
    beta_all = _sigmoid(aux)
    g_all = -jnp.exp(alog_ref[...]) * _softplus(aux + dtb_ref[...])

    ri = lax.broadcasted_iota(jnp.int32, (c, c), 0)
    ci = lax.broadcasted_iota(jnp.int32, (c, c), 1)
    causal = ri >= ci
    strict = ri > ci
    rt = lax.broadcasted_iota(jnp.int32, (tile, tile), 0)
    ct = lax.broadcasted_iota(jnp.int32, (tile, tile), 1)
    same_chunk = lax.shift_right_logical(rt, c.bit_length() - 1) == lax.shift_right_logical(ct, c.bit_length() - 1)
    gcum_all = _chunk_cumsum(((rt >= ct) & same_chunk).astype(BF16), g_all)

    dot = functools.partial(jnp.dot, preferred_element_type=F32)
    nchunk = tile // c
    pairs = []
    for hh in range(hb):
        h = hp * hb + hh
        q = _l2norm(q_all[:, hh * GDN_DK:(hh + 1) * GDN_DK]) * (GDN_DK ** -0.5)
        k = _l2norm(k_all[:, hh * GDN_DK:(hh + 1) * GDN_DK])
        v = v_all[:, hh * GDN_DV:(hh + 1) * GDN_DV]
        beta = jnp.sum(jnp.where(lane == h, beta_all, 0.0), axis=1, keepdims=True)
        gc = jnp.sum(jnp.where(lane == h + GDN_HEADS, gcum_all, 0.0), axis=1, keepdims=True)
        for i in range(nchunk):
            sl = slice(i * c, (i + 1) * c)
            qc, kc, vc, bc = q[sl], k[sl], v[sl], beta[sl]
            gcum = jnp.broadcast_to(gc[sl], (c, c))
            rel = gcum - gcum.T
            decay = jnp.where(causal, jnp.exp(jnp.where(causal, rel, 0.0)), 0.0)
            eg = jnp.exp(gcum)
            kb = kc * bc
            kcb = kc.astype(BF16)
            g_last = gcum[c - 1:c, :]
            pairs.append(dict(
                hh=hh, i=i,
                low=jnp.where(strict, _nt_dot(kb.astype(BF16), kcb) * decay, 0.0),
                attn=jnp.where(causal, _nt_dot(qc.astype(BF16), kcb) * decay, 0.0).astype(BF16),
                vb=(vc * bc).astype(BF16),
                kbe=(kb * eg).astype(BF16),
                qg=(qc * eg).astype(BF16),
                k_tail_t=(kc * jnp.exp(g_last - gcum)).T.astype(BF16),
                sdec=jnp.exp(g_last[:, :1]),
            ))
    pairs.sort(key=lambda p: (p["i"], p["hh"]))
    invs = _unit_lower_inverses([p["low"] for p in pairs])
    for p, inv in zip(pairs, invs):
        inv_b = inv.astype(BF16)
        p["u"] = dot(inv_b, p["vb"])
        p["w"] = dot(inv_b, p["kbe"]).astype(BF16)
    for p in pairs:
        hh, sl = p["hh"], slice(p["i"] * c, (p["i"] + 1) * c)
        st = state[hh]
        stb = st.astype(BF16)
        v_new = (p["u"] - dot(p["w"], stb)).astype(BF16)
        state[hh] = st * p["sdec"] + dot(p["k_tail_t"], v_new)
        o = dot(p["qg"], stb) + dot(p["attn"], v_new)
        ms = jnp.mean(o * o, axis=-1, keepdims=True)
        y = o * lax.rsqrt(ms + EPS) * gn_ref[...]
        z = z_ref[sl, hh * GDN_DV:(hh + 1) * GDN_DV].astype(F32)
        o_ref[sl, hh * GDN_DV:(hh + 1) * GDN_DV] = (y * (z * _sigmoid(z))).astype(o_ref.dtype)


def _gdn(proj, aux, conv_w, alog_row, dtb_row, gdn_norm, batch, seq):
    tile = min(GDN_TILE, seq)
    nt = seq // tile
    hb = GDN_HEADS_PER_STEP
    dk, dv = hb * GDN_DK, hb * GDN_DV
    base = 2 * RET_HEADS * RET_DK + 2 * RET_HEADS * RET_DV
    qb = base // dk
    kb = qb + GDN_HEADS // hb
    vb = (base + 2 * GDN_HEADS * GDN_DK) // dv
    zb = vb + GDN_HEADS // hb
    cvb = (2 * GDN_HEADS * GDN_DK) // dv
    row = lambda b, h, t: b * nt + t
    return pl.pallas_call(
        _gdn_kernel,
        grid=(batch, GDN_HEADS // hb, nt),
        in_specs=[
            pl.BlockSpec((tile, dk), lambda b, h, t: (row(b, h, t), qb + h)),
            pl.BlockSpec((tile, dk), lambda b, h, t: (row(b, h, t), kb + h)),
            pl.BlockSpec((tile, dv), lambda b, h, t: (row(b, h, t), vb + h)),
            pl.BlockSpec((tile, dv), lambda b, h, t: (row(b, h, t), zb + h)),
            pl.BlockSpec((tile, LANES), lambda b, h, t: (row(b, h, t), 0)),
            pl.BlockSpec((CONV_WIDTH, dk), lambda b, h, t: (0, h)),
            pl.BlockSpec((CONV_WIDTH, dk), lambda b, h, t: (0, GDN_HEADS // hb + h)),
            pl.BlockSpec((CONV_WIDTH, dv), lambda b, h, t: (0, cvb + h)),
            pl.BlockSpec((1, LANES), lambda b, h, t: (0, 0)),
            pl.BlockSpec((1, LANES), lambda b, h, t: (0, 0)),
            pl.BlockSpec((1, GDN_DV), lambda b, h, t: (0, 0)),
        ],
        out_specs=pl.BlockSpec((tile, dv), lambda b, h, t: (row(b, h, t), h)),
        out_shape=jax.ShapeDtypeStruct((batch * seq, GDN_HEADS * GDN_DV), BF16),
        scratch_shapes=[
            pltpu.VMEM((hb, GDN_DK, GDN_DV), F32),
            pltpu.VMEM((tile + SUBLANES, dk), F32),
            pltpu.VMEM((tile + SUBLANES, dk), F32),
            pltpu.VMEM((tile + SUBLANES, dv), F32),
        ],
        compiler_params=_cparams(("parallel", "parallel", "arbitrary")),
        name="gated_deltanet",
    )(proj, proj, proj, proj, aux, conv_w, conv_w, conv_w, alog_row, dtb_row, gdn_norm.reshape(1, GDN_DV))


def _rope64(x, cos, sin):
    half = SWA_HEAD_DIM // 2
    lane = lax.broadcasted_iota(jnp.int32, x.shape, x.ndim - 1)
    first = (lane % SWA_HEAD_DIM) < half
    rot = jnp.where(first, pltpu.roll(x, LANES - half, x.ndim - 1), pltpu.roll(x, half, x.ndim - 1))
    return x * cos + rot * sin


def _head_rmsnorm(x, gain, seg_ones):
    sq_hi, sq_lo = _split_bf16(x * x)
    ss = jnp.dot(sq_hi, seg_ones, preferred_element_type=F32) + jnp.dot(sq_lo, seg_ones, preferred_element_type=F32)
    return x * lax.rsqrt(ss * (1.0 / SWA_HEAD_DIM) + EPS) * gain


def _swa_kernel(sink_ref, q_ref, k_ref, v_ref, cos_ref, sin_ref, qn_ref, kn_ref, seg_ref, o_ref,
                kprev, vprev):
    i = pl.program_id(1)
    w = WINDOW
    pairs = SWA_KV_HEADS // 2
    grp = SWA_Q_HEADS // SWA_KV_HEADS
    qchunks = grp * SWA_HEAD_DIM // LANES

    @pl.when(i == 0)
    def _():
        kprev[...] = jnp.zeros_like(kprev)
        vprev[...] = jnp.zeros_like(vprev)

    cos = cos_ref[...]
    sin = sin_ref[...]
    seg = seg_ref[...]
    lane = lax.broadcasted_iota(jnp.int32, (w, LANES), 1)
    lo_mask = lane < SWA_HEAD_DIM

    k_lo, k_hi, v_lo, v_hi = [], [], [], []
    for c in range(pairs):
        kc = k_ref[:, c * LANES:(c + 1) * LANES].astype(F32)
        kc = _rope64(_head_rmsnorm(kc, kn_ref[...], seg), cos, sin)
        vc = v_ref[:, c * LANES:(c + 1) * LANES].astype(F32)
        for src, lo_list, hi_list in ((kc, k_lo, k_hi), (vc, v_lo, v_hi)):
            a_lo = jnp.where(lo_mask, src, 0.0)
            b_hi = jnp.where(lo_mask, 0.0, src)
            lo_list += [a_lo, pltpu.roll(b_hi, SWA_HEAD_DIM, 1)]
            hi_list += [pltpu.roll(a_lo, SWA_HEAD_DIM, 1), b_hi]

    qpos = lax.broadcasted_iota(jnp.int32, (w, 2 * w), 0) + w
    kpos = lax.broadcasted_iota(jnp.int32, (w, 2 * w), 1)
    rel = qpos - kpos
    first_key = jnp.where(i > 0, 0, w)
    valid = (rel >= 0) & (rel < w) & (kpos >= first_key)
    scale = SWA_HEAD_DIM ** -0.5 * LOG2E
    cos_q = jnp.concatenate([cos] * qchunks, axis=0)
    sin_q = jnp.concatenate([sin] * qchunks, axis=0)

    for kvh in range(SWA_KV_HEADS):
        q0 = kvh * grp * SWA_HEAD_DIM
        q2 = jnp.concatenate(
            [q_ref[:, q0 + c * LANES:q0 + (c + 1) * LANES].astype(F32) for c in range(qchunks)], axis=0)
        q2 = _head_rmsnorm(q2, qn_ref[...], seg)
        q2 = (_rope64(q2, cos_q, sin_q) * scale).astype(BF16)
        kl = jnp.concatenate([kprev[0, kvh], k_lo[kvh].astype(BF16)], axis=0)
        kh = jnp.concatenate([kprev[1, kvh], k_hi[kvh].astype(BF16)], axis=0)
        vl = jnp.concatenate([vprev[0, kvh], v_lo[kvh].astype(BF16)], axis=0)
        vh = jnp.concatenate([vprev[1, kvh], v_hi[kvh].astype(BF16)], axis=0)
        acc = None
        for par, kk, vv in ((0, kl, vl), (1, kh, vh)):
            sc = _nt_dot(q2, kk)
            probs, denoms = [], []
            for c in range(qchunks):
                sink = sink_ref[kvh * grp + 2 * c + par] * LOG2E
                s = jnp.where(valid, sc[c * w:(c + 1) * w, :], -jnp.inf)
                mx = jnp.maximum(jnp.max(s, axis=-1, keepdims=True), sink)
                p = jnp.exp2(s - mx)
                denoms.append(jnp.sum(p, axis=-1, keepdims=True) + jnp.exp2(sink - mx))
                probs.append(p.astype(BF16))
            pv = jnp.dot(jnp.concatenate(probs, axis=0), vv, preferred_element_type=F32)
            pv = [pv[c * w:(c + 1) * w, :] * (1.0 / denoms[c]) for c in range(qchunks)]
            acc = pv if acc is None else [a + b for a, b in zip(acc, pv)]
        for c in range(qchunks):
            o_ref[:, q0 + c * LANES:q0 + (c + 1) * LANES] = acc[c].astype(o_ref.dtype)

    for kvh in range(SWA_KV_HEADS):
        kprev[0, kvh] = k_lo[kvh].astype(BF16)
        kprev[1, kvh] = k_hi[kvh].astype(BF16)
        vprev[0, kvh] = v_lo[kvh].astype(BF16)
        vprev[1, kvh] = v_hi[kvh].astype(BF16)


def _swa(proj, cos, sin, q_norm, k_norm, sinks, batch, seq):
    w = WINDOW
    nb = seq // w
    qw = SWA_Q_HEADS * SWA_HEAD_DIM
    kvw = SWA_KV_HEADS * SWA_HEAD_DIM
    kblk = qw // kvw
    reps = LANES // SWA_HEAD_DIM
    seg = (jnp.arange(LANES)[:, None] // SWA_HEAD_DIM == jnp.arange(LANES)[None, :] // SWA_HEAD_DIM).astype(BF16)
    return pl.pallas_call(
        _swa_kernel,
        grid=(batch, nb),
        in_specs=[
            pl.BlockSpec(memory_space=pltpu.SMEM),
            pl.BlockSpec((w, qw), lambda b, i: (b * nb + i, 0)),
            pl.BlockSpec((w, kvw), lambda b, i: (b * nb + i, kblk)),
            pl.BlockSpec((w, kvw), lambda b, i: (b * nb + i, kblk + 1)),
            pl.BlockSpec((w, LANES), lambda b, i: (i, 0)),
            pl.BlockSpec((w, LANES), lambda b, i: (i, 0)),
            pl.BlockSpec((1, LANES), lambda b, i: (0, 0)),
            pl.BlockSpec((1, LANES), lambda b, i: (0, 0)),
            pl.BlockSpec((LANES, LANES), lambda b, i: (0, 0)),
        ],
        out_specs=pl.BlockSpec((w, qw), lambda b, i: (b * nb + i, 0)),
        out_shape=jax.ShapeDtypeStruct((batch * seq, qw), BF16),
        scratch_shapes=[
            pltpu.VMEM((2, SWA_KV_HEADS, w, LANES), BF16),
            pltpu.VMEM((2, SWA_KV_HEADS, w, LANES), BF16),
        ],
        compiler_params=_cparams(("parallel", "arbitrary")),
        name="swa",
    )(sinks.astype(F32), proj, proj, proj, cos, sin,
      jnp.tile(q_norm.astype(F32), reps).reshape(1, LANES),
      jnp.tile(k_norm.astype(F32), reps).reshape(1, LANES), seg)


def _route(logits):
    tm = logits.shape[0]
    lane = lax.broadcasted_iota(jnp.int32, (tm, LANES), 1)
    lane_f = lane.astype(F32)
    big = float(LANES)
    neg = -jnp.inf
    gl = jnp.where(lane < N_GROUPS, logits, neg)
    gmax = jnp.max(gl, axis=1, keepdims=True)
    gidx = jnp.min(jnp.where(gl == gmax, lane_f, big), axis=1, keepdims=True)
    g_p = 1.0 / jnp.sum(jnp.exp(gl - gmax), axis=1, keepdims=True)
    egroup = lax.shift_right_arithmetic(lane - N_GROUPS, 3).astype(F32)
    in_group = (lane >= N_GROUPS) & (lane < N_GROUPS + N_EXPERTS) & (egroup == gidx)
    el = jnp.where(in_group, logits, neg)
    emax = jnp.max(el, axis=1, keepdims=True)
    ee = jnp.exp(el - emax)
    prob = ee / jnp.sum(ee, axis=1, keepdims=True)
    pm = jnp.where(in_group, prob, -1.0)
    p1 = jnp.max(pm, axis=1, keepdims=True)
    i1 = jnp.min(jnp.where(pm == p1, lane_f, big), axis=1, keepdims=True)
    pm2 = jnp.where(lane_f == i1, -1.0, pm)
    p2 = jnp.max(pm2, axis=1, keepdims=True)
    i2 = jnp.min(jnp.where(pm2 == p2, lane_f, big), axis=1, keepdims=True)
    den = p1 + p2
    out = jnp.where(lane == 0, i1 - N_GROUPS, 0.0)
    out = jnp.where(lane == 1, i2 - N_GROUPS, out)
    out = jnp.where(lane == 2, g_p * p1 / den, out)
    out = jnp.where(lane == 3, g_p * p2 / den, out)
    return out


def _out_proj_kernel(*refs, n_act):
    acts = refs[:n_act]
    ws = refs[n_act:2 * n_act]
    x_ref, g_ref, wr_ref, x1_ref, h_ref, route_ref, y_even, y_odd = refs[2 * n_act:]
    i = pl.program_id(0)
    tm = x_ref.shape[0]

    def step(prev, cur):
        y = prev[...]
        x1_ref[...] = y
        ms = jnp.mean(y * y, axis=-1, keepdims=True)
        h = y * lax.rsqrt(ms + EPS) * g_ref[...]
        for s in range(ROW_TILES):
            h_ref[pl.ds(s, tm, stride=ROW_TILES), :] = h[:, s * LANES:(s + 1) * LANES]
        h_hi, h_lo = _split_bf16(h)
        wr = wr_ref[...]
        lg2 = jnp.dot(h_hi, wr, preferred_element_type=F32) + jnp.dot(h_lo, wr, preferred_element_type=F32)
        route_ref[...] = _route(lg2[:, :LANES] + lg2[:, LANES:])
        acc = x_ref[...]
        for a, w in zip(acts, ws):
            acc = acc + jnp.dot(a[...], w[...], preferred_element_type=F32)
        cur[...] = acc

    @pl.when(i == 0)
    def _():
        y_odd[...] = jnp.zeros_like(y_odd)

    @pl.when(i % 2 == 0)
    def _():
        step(y_odd, y_even)

    @pl.when(i % 2 == 1)
    def _():
        step(y_even, y_odd)


def _out_proj(acts, ws, x, gain, w_router, *, tm=256):
    n, d = x.shape
    tm = min(tm, n)
    n_act = len(acts)
    nt = n // tm
    const = lambda i: (0, 0)
    cur = lambda i: (jnp.minimum(i, nt - 1), 0)
    prev = lambda i: (jnp.maximum(i - 1, 0), 0)
    in_specs = [pl.BlockSpec((tm, a.shape[1]), cur) for a in acts]
    in_specs += [pl.BlockSpec(w.shape, const, pipeline_mode=pl.Buffered(1)) for w in ws]
    in_specs += [
        pl.BlockSpec((tm, d), cur),
        pl.BlockSpec((1, d), const),
        pl.BlockSpec((d, 2 * LANES), const),
    ]
    return pl.pallas_call(
        functools.partial(_out_proj_kernel, n_act=n_act),
        grid=(nt + 1,),
        in_specs=in_specs,
        out_specs=[
            pl.BlockSpec((tm, d), prev),
            pl.BlockSpec((tm * ROW_TILES, LANES), prev),
            pl.BlockSpec((tm, LANES), prev),
        ],
        out_shape=[
            jax.ShapeDtypeStruct((n, d), F32),
            jax.ShapeDtypeStruct((n * ROW_TILES, LANES), F32),
            jax.ShapeDtypeStruct((n, LANES), F32),
        ],
        scratch_shapes=[pltpu.VMEM((tm, d), F32), pltpu.VMEM((tm, d), F32)],
        compiler_params=_cparams(("arbitrary",)),
        name="out_proj_router",
    )(*acts, *ws, x, gain.reshape(1, d), w_router)


def _row_gather_start(idx_ref, base, count, src_hbm, dst, sem, *, both_queues):
    group = 8

    def body(gi, carry):
        for j in range(group):
            r = gi * group + j
            tok = idx_ref[base + r]
            pltpu.make_async_copy(src_hbm.at[pl.ds(pl.multiple_of(tok * ROW_TILES, ROW_TILES), ROW_TILES)],
                                  dst.at[:, pl.ds(r, 1), :],
                                  sem).start(priority=j % 2 if both_queues else 0)
        return carry
    lax.fori_loop(0, count // group, body, 0)


def _row_gather_wait(dst, sem):
    pltpu.make_async_copy(dst, dst, sem).wait()


def _expert_kernel(be_ref, nu_ref, nxt_ref, tok_ref, h_hbm, wg_hbm, wu_hbm, wd_hbm, o_ref,
                   xbuf0, xbuf1, xbuf2, stage_g, stage_u, stage_d, wg_b, wu_b, wd_b, gsem, wsem, *, layer):
    i = pl.program_id(0)
    bm = MOE_BM
    nu = nu_ref[0]
    last_block = pl.num_programs(0) - 1
    e = be_ref[i]
    weight_queue = 1
    xbufs = (xbuf0, xbuf1, xbuf2)

    def weight_copies(expert):
        return (pltpu.make_async_copy(wg_hbm.at[layer, expert], stage_g, wsem.at[0]),
                pltpu.make_async_copy(wu_hbm.at[layer, expert], stage_u, wsem.at[1]),
                pltpu.make_async_copy(wd_hbm.at[layer, expert], stage_d, wsem.at[2]))

    @pl.when(i == 0)
    def _():
        _row_gather_start(tok_ref, 0, bm, h_hbm, xbuf0, gsem.at[0], both_queues=False)
        _row_gather_start(tok_ref, jnp.minimum(1, last_block) * bm, bm, h_hbm, xbuf1, gsem.at[1],
                          both_queues=False)
        for cp in weight_copies(e):
            cp.start(priority=weight_queue)

    def run_block(k):
        cur, cur_sem = xbufs[k], gsem.at[k]
        mid, mid_sem = xbufs[(k + 1) % 3], gsem.at[(k + 1) % 3]
        nxt_buf, nxt_sem = xbufs[(k + 2) % 3], gsem.at[(k + 2) % 3]
        first_of_run = jnp.logical_or(i == 0, e != be_ref[jnp.maximum(i - 1, 0)])

        @pl.when(first_of_run)
        def _():
            for cp, stage, dst in zip(weight_copies(e), (stage_g, stage_u, stage_d), (wg_b, wu_b, wd_b)):
                cp.wait()
                dst[...] = stage[...].astype(BF16)
            nxt = nxt_ref[e]

            @pl.when(nxt >= 0)
            def _():
                for cp in weight_copies(nxt):
                    cp.start(priority=weight_queue)

        _row_gather_wait(cur, cur_sem)
        base = jnp.minimum(i + 2, last_block) * bm
        for r in range(bm):
            tok = tok_ref[base + r]
            pltpu.make_async_copy(h_hbm.at[pl.ds(pl.multiple_of(tok * ROW_TILES, ROW_TILES), ROW_TILES)],
                                  nxt_buf.at[:, pl.ds(r, 1), :], nxt_sem).start()
        x = jnp.concatenate([cur[s].astype(BF16) for s in range(ROW_TILES)], axis=1)
        g = jnp.dot(x, wg_b[...], preferred_element_type=F32)
        u = jnp.dot(x, wu_b[...], preferred_element_type=F32)
        hid = (g * _sigmoid(g) * u).astype(BF16)
        y = jnp.dot(hid, wd_b[...], preferred_element_type=F32)
        for s in range(ROW_TILES):
            o_ref[pl.ds(s, bm, stride=ROW_TILES), :] = y[:, s * LANES:(s + 1) * LANES]

        @pl.when(i + 1 >= nu)
        def _():
            _row_gather_wait(mid, mid_sem)
            _row_gather_wait(nxt_buf, nxt_sem)

    for k in range(3):
        pl.when(jnp.logical_and(i < nu, i % 3 == k))(functools.partial(run_block, k))

    @pl.when(i >= nu)
    def _():
        o_ref[...] = jnp.zeros_like(o_ref)


def _expert_mlp(h_tiled, tok_pad, block_expert, n_used, next_expert, w_gate, w_up, w_down, layer):
    n_pad = tok_pad.shape[0]
    bm = MOE_BM
    _, _, d, ff = w_gate.shape
    any_spec = pl.BlockSpec(memory_space=pl.ANY)
    grid_spec = pltpu.PrefetchScalarGridSpec(
        num_scalar_prefetch=4,
        grid=(n_pad // bm,),
        in_specs=[any_spec, any_spec, any_spec, any_spec],
        out_specs=pl.BlockSpec((bm * ROW_TILES, LANES), lambda i, *_: (i, 0)),
        scratch_shapes=[
            pltpu.VMEM((ROW_TILES, bm, LANES), F32), pltpu.VMEM((ROW_TILES, bm, LANES), F32),
            pltpu.VMEM((ROW_TILES, bm, LANES), F32),
            pltpu.VMEM((d, ff), F32), pltpu.VMEM((d, ff), F32), pltpu.VMEM((ff, d), F32),
            pltpu.VMEM((d, ff), BF16), pltpu.VMEM((d, ff), BF16), pltpu.VMEM((ff, d), BF16),
            pltpu.SemaphoreType.DMA((3,)),
            pltpu.SemaphoreType.DMA((3,)),
        ],
    )
    return pl.pallas_call(
        functools.partial(_expert_kernel, layer=layer),
        grid_spec=grid_spec,
        out_shape=jax.ShapeDtypeStruct((n_pad * ROW_TILES, LANES), F32),
        compiler_params=_cparams(("arbitrary",)),
        name="expert_mlp",
    )(block_expert, n_used, next_expert, tok_pad, h_tiled.reshape(-1, 1, LANES), w_gate, w_up, w_down)


def _combine_kernel(dest_ref, x1_ref, route_ref, y_hbm, o_ref, ybuf, sem):
    i = pl.program_id(0)
    nsteps = pl.num_programs(0)
    tm = x1_ref.shape[0]
    slot = i % 2

    def start(step, sl):
        for k in range(2):
            _row_gather_start(dest_ref, k * (nsteps * tm) + step * tm, tm, y_hbm, ybuf.at[sl, k], sem.at[sl],
                              both_queues=True)

    @pl.when(i == 0)
    def _():
        start(0, 0)

    @pl.when(i + 1 < nsteps)
    def _():
        start(i + 1, 1 - slot)

    _row_gather_wait(ybuf.at[slot], sem.at[slot])
    route = route_ref[...]
    g0 = jnp.broadcast_to(route[:, 2:3], (tm, LANES))
    g1 = jnp.broadcast_to(route[:, 3:4], (tm, LANES))
    for s in range(ROW_TILES):
        o_ref[:, s * LANES:(s + 1) * LANES] = (x1_ref[:, s * LANES:(s + 1) * LANES]
                                               + g0 * ybuf[slot, 0, s] + g1 * ybuf[slot, 1, s])


def _combine(x1, route, dest, y_tiled, *, tm=256):
    n, d = x1.shape
    tm = min(tm, n)
    grid_spec = pltpu.PrefetchScalarGridSpec(
        num_scalar_prefetch=1,
        grid=(n // tm,),
        in_specs=[
            pl.BlockSpec((tm, d), lambda i, *_: (i, 0)),
            pl.BlockSpec((tm, LANES), lambda i, *_: (i, 0)),
            pl.BlockSpec(memory_space=pl.ANY),
        ],
        out_specs=pl.BlockSpec((tm, d), lambda i, *_: (i, 0)),
        scratch_shapes=[
            pltpu.VMEM((2, 2, ROW_TILES, tm, LANES), F32),
            pltpu.SemaphoreType.DMA((2,)),
        ],
    )
    return pl.pallas_call(
        _combine_kernel,
        grid_spec=grid_spec,
        out_shape=jax.ShapeDtypeStruct((n, d), F32),
        compiler_params=_cparams(("arbitrary",)),
        name="moe_combine",
    )(dest, x1, route, y_tiled.reshape(-1, 1, LANES))


def _moe(x1, h_tiled, route, w_gate, w_up, w_down, layer):
    n, d = x1.shape
    bm = MOE_BM
    n_assign = 2 * n
    expert_id = route[:, 0:2].astype(jnp.int32).reshape(-1)
    onehot = (expert_id[:, None] == jnp.arange(N_EXPERTS, dtype=jnp.int32)[None, :]).astype(jnp.int32)
    csum = jnp.cumsum(onehot, axis=0)
    rank = jnp.sum(csum * onehot, axis=1) - 1
    counts = csum[-1]
    padded = (counts + bm - 1) // bm * bm
    pend = jnp.cumsum(padded)
    pstart = pend - padded
    dest = (jnp.sum(onehot * pstart[None, :], axis=1) + rank).astype(jnp.int32)
    n_pad = (n_assign + N_EXPERTS * (bm - 1) + bm - 1) // bm * bm
    n_blk = n_pad // bm
    token_id = jnp.arange(n_assign, dtype=jnp.int32) // 2
    tok_pad = (jnp.arange(n_pad, dtype=jnp.int32) % n).at[dest].set(token_id)
    block_start = jnp.arange(n_blk, dtype=jnp.int32) * bm
    block_expert = jnp.minimum(jnp.sum(pend[None, :] <= block_start[:, None], axis=1), N_EXPERTS - 1)
    n_used = (pend[-1] // bm).astype(jnp.int32).reshape(1)
    ids = jnp.arange(N_EXPERTS, dtype=jnp.int32)
    cand = jnp.where((counts[None, :] > 0) & (ids[None, :] > ids[:, None]), ids[None, :], N_EXPERTS)
    next_expert = jnp.min(cand, axis=1)
    next_expert = jnp.where(next_expert >= N_EXPERTS, -1, next_expert).astype(jnp.int32)
    y_tiled = _expert_mlp(h_tiled, tok_pad, block_expert.astype(jnp.int32), n_used, next_expert,
                          w_gate, w_up, w_down, layer)
    dest_by_choice = dest.reshape(n, 2).T.reshape(-1).astype(jnp.int32)
    return _combine(x1, route, dest_by_choice, y_tiled)


def _router_weights(w_group, w_expert):
    d = w_group.shape[0]
    pad = jnp.zeros((d, LANES - N_GROUPS - N_EXPERTS), F32)
    w = jnp.concatenate([w_group.astype(F32), w_expert.astype(F32), pad], axis=1)
    w_hi = w.astype(BF16)
    w_lo = (w - w_hi.astype(F32)).astype(BF16)
    return jnp.concatenate([w_hi, w_lo], axis=1)


def kernel(x, norm_mix, norm_ffn, even_w_in, ret_norm, gdn_conv, gdn_a_log, gdn_dt_bias, gdn_norm,
           even_w_out, odd_w_in, q_norm, k_norm, attn_sinks, odd_w_out, router_group, router_expert,
           expert_w_gate, expert_w_up, expert_w_down):
    batch, seq, d = x.shape
    n = batch * seq
    xt = x.reshape(n, d)

    w_in = even_w_in[0]
    w_main = w_in[:, :EVEN_MAIN].astype(BF16)
    w_aux = jnp.pad(w_in[:, EVEN_MAIN:], ((0, 0), (0, LANES - 2 * GDN_HEADS))).astype(BF16)
    proj, aux = _norm_proj(xt, norm_mix[0], w_main, w_aux, tn=EVEN_MAIN // 8)
    cos_r, sin_r = _rope_tables(seq, RET_DK // 2)
    log_gamma = jnp.log1p(-jnp.exp2(-5.0 - jnp.arange(RET_HEADS, dtype=F32)))
    log_gamma = jnp.broadcast_to(log_gamma[:, None, None], (RET_HEADS, 1, LANES))
    o_ret = _retention(proj, cos_r, sin_r, log_gamma, ret_norm[0].astype(F32), batch, seq)
    lane_pad = (GDN_HEADS, LANES - 2 * GDN_HEADS)
    alog_row = jnp.pad(gdn_a_log[0].astype(F32), lane_pad).reshape(1, LANES)
    dtb_row = jnp.pad(gdn_dt_bias[0].astype(F32), lane_pad).reshape(1, LANES)
    o_gdn = _gdn(proj, aux, gdn_conv[0].astype(F32), alog_row, dtb_row, gdn_norm[0].astype(F32), batch, seq)
    w_out = even_w_out[0].astype(BF16)
    split = RET_HEADS * RET_DV
    x1, h2, route = _out_proj([o_ret, o_gdn], [w_out[:split], w_out[split:]], xt, norm_ffn[0],
                              _router_weights(router_group[0], router_expert[0]))
    xt = _moe(x1, h2, route, expert_w_gate, expert_w_up, expert_w_down, 0)

    proj = _norm_proj(xt, norm_mix[1], odd_w_in[0].astype(BF16), tn=ODD_IN // 2)
    cos_s, sin_s = _rope_tables(seq, SWA_HEAD_DIM // 2)
    o_swa = _swa(proj, cos_s, sin_s, q_norm[0], k_norm[0], attn_sinks[0], batch, seq)
    x1, h2, route = _out_proj([o_swa], [odd_w_out[0].astype(BF16)], xt, norm_ffn[1],
                              _router_weights(router_group[1], router_expert[1]))
    xt = _moe(x1, h2, route, expert_w_gate, expert_w_up, expert_w_down, 1)
    return xt.reshape(batch, seq, d)
```

```python
import functools

import jax
import jax.numpy as jnp
from jax import lax
from jax.experimental import pallas as pl
from jax.experimental.pallas import tpu as pltpu

F32 = jnp.float32
BF16 = jnp.bfloat16

D_MODEL = 2048
RET_HEADS = 8
RET_DK = 128
RET_DV = 256
GDN_HEADS = 8
GDN_DK = 128
GDN_DV = 256
CONV_WIDTH = 4
SWA_Q_HEADS = 32
SWA_KV_HEADS = 4
SWA_HEAD_DIM = 64
WINDOW = 128
ROPE_THETA = 10000.0
N_GROUPS = 4
EXPERTS_PER_GROUP = 8
N_EXPERTS = N_GROUPS * EXPERTS_PER_GROUP
D_FF_EXPERT = 768
EPS = 1e-6
LOG2E = 1.4426950408889634

LANES = 128
SUBLANES = 8
VMEM_LIMIT = 56 * 1024 * 1024

EVEN_MAIN = 2 * RET_HEADS * RET_DK + 2 * RET_HEADS * RET_DV + 2 * GDN_HEADS * GDN_DK + 2 * GDN_HEADS * GDN_DV
ODD_IN = (SWA_Q_HEADS + 2 * SWA_KV_HEADS) * SWA_HEAD_DIM

RET_TILE = 256
RET_HEADS_PER_STEP = 4
GDN_TILE = 256
GDN_CHUNK = 128
GDN_HEADS_PER_STEP = 8
MOE_BM = 256
ROW_TILES = D_MODEL // LANES


def _cparams(sem):
    return pltpu.CompilerParams(dimension_semantics=sem, vmem_limit_bytes=VMEM_LIMIT)


def _sigmoid(x):
    return 1.0 / (1.0 + jnp.exp(-x))


def _nt_dot(a, b):
    return lax.dot_general(a, b, (((1,), (1,)), ((), ())), preferred_element_type=F32)


def _tn_dot(a, b):
    return lax.dot_general(a, b, (((0,), (0,)), ((), ())), preferred_element_type=F32)


def _rope_table_kernel(cos_ref, sin_ref, *, half):
    rows = cos_ref.shape[0]
    r0 = pl.program_id(0) * rows
    pos = (lax.broadcasted_iota(jnp.int32, (rows, LANES), 0) + r0).astype(F32)
    lane = lax.broadcasted_iota(jnp.int32, (rows, LANES), 1)
    fi = (lane % half).astype(F32)
    inv_freq = jnp.exp(-(fi / half) * jnp.log(ROPE_THETA))
    ang = pos * inv_freq
    first = (lane % (2 * half)) < half
    cos_ref[...] = jnp.cos(ang)
    sin_ref[...] = jnp.where(first, -jnp.sin(ang), jnp.sin(ang))


def _rope_tables(seq, half):
    rows = min(seq, 1024)
    return pl.pallas_call(
        functools.partial(_rope_table_kernel, half=half),
        grid=(seq // rows,),
        out_specs=[pl.BlockSpec((rows, LANES), lambda i: (i, 0))] * 2,
        out_shape=[jax.ShapeDtypeStruct((seq, LANES), F32)] * 2,
        compiler_params=_cparams(("arbitrary",)),
        name="rope_tables",
    )()


def _norm_proj_kernel(x_ref, g_ref, w_ref, *rest, with_aux):
    if with_aux:
        waux_ref, o_ref, aux_ref, h_scr = rest
    else:
        o_ref, h_scr = rest
    j = pl.program_id(1)

    @pl.when(j == 0)
    def _():
        x = x_ref[...]
        ms = jnp.mean(x * x, axis=-1, keepdims=True)
        h = x * lax.rsqrt(ms + EPS) * g_ref[...]
        h_scr[...] = h.astype(BF16)
        if with_aux:
            aux_ref[...] = jnp.dot(h_scr[...], waux_ref[...], preferred_element_type=F32)

    o_ref[...] = jnp.dot(h_scr[...], w_ref[...], preferred_element_type=F32).astype(o_ref.dtype)


def _norm_proj(x, gain, w, w_aux=None, *, tm=1024, tn=1024):
    n, d = x.shape
    n_out = w.shape[1]
    tm = min(tm, n)
    tn = min(tn, n_out)
    with_aux = w_aux is not None
    in_specs = [
        pl.BlockSpec((tm, d), lambda i, j: (i, 0)),
        pl.BlockSpec((1, d), lambda i, j: (0, 0)),
        pl.BlockSpec((d, tn), lambda i, j: (0, j)),
    ]
    out_specs = [pl.BlockSpec((tm, tn), lambda i, j: (i, j))]
    out_shape = [jax.ShapeDtypeStruct((n, n_out), BF16)]
    args = [x, gain.reshape(1, d), w]
    if with_aux:
        in_specs.append(pl.BlockSpec((d, LANES), lambda i, j: (0, 0)))
        out_specs.append(pl.BlockSpec((tm, LANES), lambda i, j: (i, 0)))
        out_shape.append(jax.ShapeDtypeStruct((n, LANES), F32))
        args.append(w_aux)
    res = pl.pallas_call(
        functools.partial(_norm_proj_kernel, with_aux=with_aux),
        grid=(n // tm, n_out // tn),
        in_specs=in_specs,
        out_specs=out_specs,
        out_shape=out_shape,
        scratch_shapes=[pltpu.VMEM((tm, d), BF16)],
        compiler_params=_cparams(("parallel", "arbitrary")),
        name="norm_proj",
    )(*args)
    return res if with_aux else res[0]


def _rope128(x, cos, sin):
    return x * cos + pltpu.roll(x, RET_DK // 2, 1) * sin


def _retention_kernel(q_ref, k_ref, v_ref, g_ref, cos_ref, sin_ref, lg_ref, gn_ref, o_ref,
                      state, dmat, qdec, kdec):
    t = pl.program_id(2)
    tile = q_ref.shape[0]
    hb = RET_HEADS_PER_STEP
    dot = functools.partial(jnp.dot, preferred_element_type=F32)
    lgs = [lg_ref[hh][:, :1] for hh in range(hb)]

    @pl.when(t == 0)
    def _():
        state[...] = jnp.zeros_like(state)
        ri = lax.broadcasted_iota(jnp.int32, (tile, tile), 0)
        ci = lax.broadcasted_iota(jnp.int32, (tile, tile), 1)
        causal = ri >= ci
        rel = jnp.where(causal, ri - ci, 0).astype(F32)
        pos = lax.broadcasted_iota(jnp.int32, (tile, RET_DK), 0).astype(F32)
        for hh, lg in enumerate(lgs):
            dmat[hh] = jnp.where(causal, jnp.exp(lg * rel), 0.0)
            qdec[hh] = jnp.exp(lg * (pos + 1.0))
            kdec[hh] = jnp.exp(lg * (tile - 1.0 - pos))

    cos = cos_ref[...]
    sin = sin_ref[...]
    heads = range(hb)
    qs = [_rope128(q_ref[:, hh * RET_DK:(hh + 1) * RET_DK].astype(F32), cos, sin) for hh in heads]
    ks = [_rope128(k_ref[:, hh * RET_DK:(hh + 1) * RET_DK].astype(F32), cos, sin) * (RET_DK ** -0.5)
          for hh in heads]
    vs = [v_ref[:, hh * RET_DV:(hh + 1) * RET_DV] for hh in heads]
    ss = [_nt_dot(qs[hh].astype(BF16), ks[hh].astype(BF16)) for hh in heads]
    sts = [state[hh] for hh in heads]
    cross = [dot((qs[hh] * qdec[hh]).astype(BF16), sts[hh].astype(BF16)) for hh in heads]
    upd = [_tn_dot((ks[hh] * kdec[hh]).astype(BF16), vs[hh]) for hh in heads]
    for hh in heads:
        state[hh] = sts[hh] * jnp.exp(lgs[hh] * float(tile)) + upd[hh]
    for hh in heads:
        o = dot((ss[hh] * dmat[hh]).astype(BF16), vs[hh]) + cross[hh]
        ms = jnp.mean(o * o, axis=-1, keepdims=True)
        y = o * lax.rsqrt(ms + EPS) * gn_ref[...]
        gate = g_ref[:, hh * RET_DV:(hh + 1) * RET_DV].astype(F32)
        o_ref[:, hh * RET_DV:(hh + 1) * RET_DV] = (y * (gate * _sigmoid(gate))).astype(o_ref.dtype)


def _retention(proj, cos, sin, log_gamma, ret_norm, batch, seq):
    tile = min(RET_TILE, seq)
    nt = seq // tile
    hb = RET_HEADS_PER_STEP
    dk, dv = hb * RET_DK, hb * RET_DV
    groups = RET_HEADS // hb
    qb = 0
    kb = groups
    vb = (2 * RET_HEADS * RET_DK) // dv
    gb = vb + groups
    row = lambda b, h, t: b * nt + t
    return pl.pallas_call(
        _retention_kernel,
        grid=(batch, groups, nt),
        in_specs=[
            pl.BlockSpec((tile, dk), lambda b, h, t: (row(b, h, t), qb + h)),
            pl.BlockSpec((tile, dk), lambda b, h, t: (row(b, h, t), kb + h)),
            pl.BlockSpec((tile, dv), lambda b, h, t: (row(b, h, t), vb + h)),
            pl.BlockSpec((tile, dv), lambda b, h, t: (row(b, h, t), gb + h)),
            pl.BlockSpec((tile, LANES), lambda b, h, t: (t, 0)),
            pl.BlockSpec((tile, LANES), lambda b, h, t: (t, 0)),
            pl.BlockSpec((hb, 1, LANES), lambda b, h, t: (h, 0, 0)),
            pl.BlockSpec((1, RET_DV), lambda b, h, t: (0, 0)),
        ],
        out_specs=pl.BlockSpec((tile, dv), lambda b, h, t: (row(b, h, t), h)),
        out_shape=jax.ShapeDtypeStruct((batch * seq, RET_HEADS * RET_DV), BF16),
        scratch_shapes=[
            pltpu.VMEM((hb, RET_DK, RET_DV), F32),
            pltpu.VMEM((hb, tile, tile), F32),
            pltpu.VMEM((hb, tile, RET_DK), F32),
            pltpu.VMEM((hb, tile, RET_DK), F32),
        ],
        compiler_params=_cparams(("parallel", "parallel", "arbitrary")),
        name="retention",
    )(proj, proj, proj, proj, cos, sin, log_gamma, ret_norm.reshape(1, RET_DV))


def _conv_silu(xbuf, x_ref, w_ref):
    tile = x_ref.shape[0]
    xbuf[SUBLANES:SUBLANES + tile, :] = x_ref[...].astype(F32)
    w = w_ref[...]
    base = SUBLANES - (CONV_WIDTH - 1)
    acc = xbuf[base:base + tile, :] * w[0:1, :]
    for j in range(1, CONV_WIDTH):
        acc = acc + xbuf[base + j:base + j + tile, :] * w[j:j + 1, :]
    xbuf[0:SUBLANES, :] = xbuf[tile:tile + SUBLANES, :]
    return acc * _sigmoid(acc)


def _l2norm(x):
    return x * lax.rsqrt(jnp.sum(x * x, axis=-1, keepdims=True) + EPS)


def _softplus(x):
    return jnp.maximum(x, 0.0) + jnp.log1p(jnp.exp(-jnp.abs(x)))


def _unit_lower_inverses(lows):
    c = lows[0].shape[0]
    dot = functools.partial(jnp.dot, preferred_element_type=F32)
    eye = (lax.broadcasted_iota(jnp.int32, (c, c), 0) == lax.broadcasted_iota(jnp.int32, (c, c), 1)).astype(F32)
    invs = [eye - low for low in lows]
    powers = [_split_bf16(low) for low in lows]
    span = 2
    while span < c:
        powers = [_split_bf16(_dot_split(hi, lo, hi, lo)) for hi, lo in powers]
        invs = [inv + _dot_split(*_split_bf16(inv), hi, lo) for inv, (hi, lo) in zip(invs, powers)]
        span *= 2
    return invs


def _split_bf16(a):
    hi = a.astype(BF16)
    return hi, (a - hi.astype(F32)).astype(BF16)


def _dot_split(a_hi, a_lo, b_hi, b_lo):
    dot = functools.partial(jnp.dot, preferred_element_type=F32)
    return dot(a_hi, b_hi) + dot(a_hi, b_lo) + dot(a_lo, b_hi)


def _chunk_cumsum(tril_b, g):
    dot = functools.partial(jnp.dot, preferred_element_type=F32)
    g1 = g.astype(BF16)
    r1 = g - g1.astype(F32)
    g2 = r1.astype(BF16)
    g3 = (r1 - g2.astype(F32)).astype(BF16)
    return dot(tril_b, g1) + dot(tril_b, g2) + dot(tril_b, g3)


def _gdn_kernel(xq_ref, xk_ref, xv_ref, z_ref, aux_ref, wq_ref, wk_ref, wv_ref, alog_ref, dtb_ref,
                gn_ref, o_ref, state, qbuf, kbuf, vbuf):
    hp = pl.program_id(1)
    t = pl.program_id(2)
    tile = xq_ref.shape[0]
    c = GDN_CHUNK
    hb = GDN_HEADS_PER_STEP

    @pl.when(t == 0)
    def _():
        state[...] = jnp.zeros_like(state)
        qbuf[0:SUBLANES, :] = jnp.zeros((SUBLANES, qbuf.shape[1]), F32)
        kbuf[0:SUBLANES, :] = jnp.zeros((SUBLANES, kbuf.shape[1]), F32)
        vbuf[0:SUBLANES, :] = jnp.zeros((SUBLANES, vbuf.shape[1]), F32)

    q_all = _conv_silu(qbuf, xq_ref, wq_ref)
    k_all = _conv_silu(kbuf, xk_ref, wk_ref)
    v_all = _conv_silu(vbuf, xv_ref, wv_ref)

    aux = aux_ref[...]
    lane = lax.broadcasted_iota(jnp.int32, (tile, LANES), 1)
    beta_all = _sigmoid(aux)
    g_all = -jnp.exp(alog_ref[...]) * _softplus(aux + dtb_ref[...])

    ri = lax.broadcasted_iota(jnp.int32, (c, c), 0)
    ci = lax.broadcasted_iota(jnp.int32, (c, c), 1)
    causal = ri >= ci
    strict = ri > ci
    rt = lax.broadcasted_iota(jnp.int32, (tile, tile), 0)
    ct = lax.broadcasted_iota(jnp.int32, (tile, tile), 1)
    same_chunk = lax.shift_right_logical(rt, c.bit_length() - 1) == lax.shift_right_logical(ct, c.bit_length() - 1)
    gcum_all = _chunk_cumsum(((rt >= ct) & same_chunk).astype(BF16), g_all)

    dot = functools.partial(jnp.dot, preferred_element_type=F32)
    nchunk = tile // c
    pairs = []
    for hh in range(hb):
        h = hp * hb + hh
        q = _l2norm(q_all[:, hh * GDN_DK:(hh + 1) * GDN_DK]) * (GDN_DK ** -0.5)
        k = _l2norm(k_all[:, hh * GDN_DK:(hh + 1) * GDN_DK])
        v = v_all[:, hh * GDN_DV:(hh + 1) * GDN_DV]
        beta = jnp.sum(jnp.where(lane == h, beta_all, 0.0), axis=1, keepdims=True)
        gc = jnp.sum(jnp.where(lane == h + GDN_HEADS, gcum_all, 0.0), axis=1, keepdims=True)
        for i in range(nchunk):
            sl = slice(i * c, (i + 1) * c)
            qc, kc, vc, bc = q[sl], k[sl], v[sl], beta[sl]
            gcum = jnp.broadcast_to(gc[sl], (c, c))
            rel = gcum - gcum.T
            decay = jnp.where(causal, jnp.exp(jnp.where(causal, rel, 0.0)), 0.0)
            eg = jnp.exp(gcum)
            kb = kc * bc
            kcb = kc.astype(BF16)
            g_last = gcum[c - 1:c, :]
            pairs.append(dict(
                hh=hh, i=i,
                low=jnp.where(strict, _nt_dot(kb.astype(BF16), kcb) * decay, 0.0),
                attn=jnp.where(causal, _nt_dot(qc.astype(BF16), kcb) * decay, 0.0).astype(BF16),
                vb=(vc * bc).astype(BF16),
                kbe=(kb * eg).astype(BF16),
                qg=(qc * eg).astype(BF16),
                k_tail_t=(kc * jnp.exp(g_last - gcum)).T.astype(BF16),
                sdec=jnp.exp(g_last[:, :1]),
            ))
    pairs.sort(key=lambda p: (p["i"], p["hh"]))
    invs = _unit_lower_inverses([p["low"] for p in pairs])
    for p, inv in zip(pairs, invs):
        inv_b = inv.astype(BF16)
        p["u"] = dot(inv_b, p["vb"])
        p["w"] = dot(inv_b, p["kbe"]).astype(BF16)
    for p in pairs:
        hh, sl = p["hh"], slice(p["i"] * c, (p["i"] + 1) * c)
        st = state[hh]
        stb = st.astype(BF16)
        v_new = (p["u"] - dot(p["w"], stb)).astype(BF16)
        state[hh] = st * p["sdec"] + dot(p["k_tail_t"], v_new)
        o = dot(p["qg"], stb) + dot(p["attn"], v_new)
        ms = jnp.mean(o * o, axis=-1, keepdims=True)
        y = o * lax.rsqrt(ms + EPS) * gn_ref[...]
        z = z_ref[sl, hh * GDN_DV:(hh + 1) * GDN_DV].astype(F32)
        o_ref[sl, hh * GDN_DV:(hh + 1) * GDN_DV] = (y * (z * _sigmoid(z))).astype(o_ref.dtype)


def _gdn(proj, aux, conv_w, alog_row, dtb_row, gdn_norm, batch, seq):
    tile = min(GDN_TILE, seq)
    nt = seq // tile
    hb = GDN_HEADS_PER_STEP
    dk, dv = hb * GDN_DK, hb * GDN_DV
    base = 2 * RET_HEADS * RET_DK + 2 * RET_HEADS * RET_DV
    qb = base // dk
    kb = qb + GDN_HEADS // hb
    vb = (base + 2 * GDN_HEADS * GDN_DK) // dv
    zb = vb + GDN_HEADS // hb
    cvb = (2 * GDN_HEADS * GDN_DK) // dv
    row = lambda b, h, t: b * nt + t
    return pl.pallas_call(
        _gdn_kernel,
        grid=(batch, GDN_HEADS // hb, nt),
        in_specs=[
            pl.BlockSpec((tile, dk), lambda b, h, t: (row(b, h, t), qb + h)),
            pl.BlockSpec((tile, dk), lambda b, h, t: (row(b, h, t), kb + h)),
            pl.BlockSpec((tile, dv), lambda b, h, t: (row(b, h, t), vb + h)),
            pl.BlockSpec((tile, dv), lambda b, h, t: (row(b, h, t), zb + h)),
            pl.BlockSpec((tile, LANES), lambda b, h, t: (row(b, h, t), 0)),
            pl.BlockSpec((CONV_WIDTH, dk), lambda b, h, t: (0, h)),
            pl.BlockSpec((CONV_WIDTH, dk), lambda b, h, t: (0, GDN_HEADS // hb + h)),
            pl.BlockSpec((CONV_WIDTH, dv), lambda b, h, t: (0, cvb + h)),
            pl.BlockSpec((1, LANES), lambda b, h, t: (0, 0)),
            pl.BlockSpec((1, LANES), lambda b, h, t: (0, 0)),
            pl.BlockSpec((1, GDN_DV), lambda b, h, t: (0, 0)),
        ],
        out_specs=pl.BlockSpec((tile, dv), lambda b, h, t: (row(b, h, t), h)),
        out_shape=jax.ShapeDtypeStruct((batch * seq, GDN_HEADS * GDN_DV), BF16),
        scratch_shapes=[
            pltpu.VMEM((hb, GDN_DK, GDN_DV), F32),
            pltpu.VMEM((tile + SUBLANES, dk), F32),
            pltpu.VMEM((tile + SUBLANES, dk), F32),
            pltpu.VMEM((tile + SUBLANES, dv), F32),
        ],
        compiler_params=_cparams(("parallel", "parallel", "arbitrary")),
        name="gated_deltanet",
    )(proj, proj, proj, proj, aux, conv_w, conv_w, conv_w, alog_row, dtb_row, gdn_norm.reshape(1, GDN_DV))


def _rope64(x, cos, sin):
    half = SWA_HEAD_DIM // 2
    lane = lax.broadcasted_iota(jnp.int32, x.shape, x.ndim - 1)
    first = (lane % SWA_HEAD_DIM) < half
    rot = jnp.where(first, pltpu.roll(x, LANES - half, x.ndim - 1), pltpu.roll(x, half, x.ndim - 1))
    return x * cos + rot * sin


def _head_rmsnorm(x, gain, seg_ones):
    sq_hi, sq_lo = _split_bf16(x * x)
    ss = jnp.dot(sq_hi, seg_ones, preferred_element_type=F32) + jnp.dot(sq_lo, seg_ones, preferred_element_type=F32)
    return x * lax.rsqrt(ss * (1.0 / SWA_HEAD_DIM) + EPS) * gain


def _swa_kernel(sink_ref, q_ref, k_ref, v_ref, cos_ref, sin_ref, qn_ref, kn_ref, seg_ref, o_ref,
                kprev, vprev):
    i = pl.program_id(1)
    w = WINDOW
    pairs = SWA_KV_HEADS // 2
    grp = SWA_Q_HEADS // SWA_KV_HEADS
    qchunks = grp * SWA_HEAD_DIM // LANES

    @pl.when(i == 0)
    def _():
        kprev[...] = jnp.zeros_like(kprev)
        vprev[...] = jnp.zeros_like(vprev)

    cos = cos_ref[...]
    sin = sin_ref[...]
    seg = seg_ref[...]
    lane = lax.broadcasted_iota(jnp.int32, (w, LANES), 1)
    lo_mask = lane < SWA_HEAD_DIM

    k_lo, k_hi, v_lo, v_hi = [], [], [], []
    for c in range(pairs):
        kc = k_ref[:, c * LANES:(c + 1) * LANES].astype(F32)
        kc = _rope64(_head_rmsnorm(kc, kn_ref[...], seg), cos, sin)
        vc = v_ref[:, c * LANES:(c + 1) * LANES].astype(F32)
        for src, lo_list, hi_list in ((kc, k_lo, k_hi), (vc, v_lo, v_hi)):
            a_lo = jnp.where(lo_mask, src, 0.0)
            b_hi = jnp.where(lo_mask, 0.0, src)
            lo_list += [a_lo, pltpu.roll(b_hi, SWA_HEAD_DIM, 1)]
            hi_list += [pltpu.roll(a_lo, SWA_HEAD_DIM, 1), b_hi]

    qpos = lax.broadcasted_iota(jnp.int32, (w, 2 * w), 0) + w
    kpos = lax.broadcasted_iota(jnp.int32, (w, 2 * w), 1)
    rel = qpos - kpos
    first_key = jnp.where(i > 0, 0, w)
    valid = (rel >= 0) & (rel < w) & (kpos >= first_key)
    scale = SWA_HEAD_DIM ** -0.5 * LOG2E
    cos_q = jnp.concatenate([cos] * qchunks, axis=0)
    sin_q = jnp.concatenate([sin] * qchunks, axis=0)

    for kvh in range(SWA_KV_HEADS):
        q0 = kvh * grp * SWA_HEAD_DIM
        q2 = jnp.concatenate(
            [q_ref[:, q0 + c * LANES:q0 + (c + 1) * LANES].astype(F32) for c in range(qchunks)], axis=0)
        q2 = _head_rmsnorm(q2, qn_ref[...], seg)
        q2 = (_rope64(q2, cos_q, sin_q) * scale).astype(BF16)
        kl = jnp.concatenate([kprev[0, kvh], k_lo[kvh].astype(BF16)], axis=0)
        kh = jnp.concatenate([kprev[1, kvh], k_hi[kvh].astype(BF16)], axis=0)
        vl = jnp.concatenate([vprev[0, kvh], v_lo[kvh].astype(BF16)], axis=0)
        vh = jnp.concatenate([vprev[1, kvh], v_hi[kvh].astype(BF16)], axis=0)
        acc = None
        for par, kk, vv in ((0, kl, vl), (1, kh, vh)):
            sc = _nt_dot(q2, kk)
            probs, denoms = [], []
            for c in range(qchunks):
                sink = sink_ref[kvh * grp + 2 * c + par] * LOG2E
                s = jnp.where(valid, sc[c * w:(c + 1) * w, :], -jnp.inf)
                mx = jnp.maximum(jnp.max(s, axis=-1, keepdims=True), sink)
                p = jnp.exp2(s - mx)
                denoms.append(jnp.sum(p, axis=-1, keepdims=True) + jnp.exp2(sink - mx))
                probs.append(p.astype(BF16))
            pv = jnp.dot(jnp.concatenate(probs, axis=0), vv, preferred_element_type=F32)
            pv = [pv[c * w:(c + 1) * w, :] * (1.0 / denoms[c]) for c in range(qchunks)]
            acc = pv if acc is None else [a + b for a, b in zip(acc, pv)]
        for c in range(qchunks):
            o_ref[:, q0 + c * LANES:q0 + (c + 1) * LANES] = acc[c].astype(o_ref.dtype)

    for kvh in range(SWA_KV_HEADS):
        kprev[0, kvh] = k_lo[kvh].astype(BF16)
        kprev[1, kvh] = k_hi[kvh].astype(BF16)
        vprev[0, kvh] = v_lo[kvh].astype(BF16)
        vprev[1, kvh] = v_hi[kvh].astype(BF16)


def _swa(proj, cos, sin, q_norm, k_norm, sinks, batch, seq):
    w = WINDOW
    nb = seq // w
    qw = SWA_Q_HEADS * SWA_HEAD_DIM
    kvw = SWA_KV_HEADS * SWA_HEAD_DIM
    kblk = qw // kvw
    reps = LANES // SWA_HEAD_DIM
    seg = (jnp.arange(LANES)[:, None] // SWA_HEAD_DIM == jnp.arange(LANES)[None, :] // SWA_HEAD_DIM).astype(BF16)
    return pl.pallas_call(
        _swa_kernel,
        grid=(batch, nb),
        in_specs=[
            pl.BlockSpec(memory_space=pltpu.SMEM),
            pl.BlockSpec((w, qw), lambda b, i: (b * nb + i, 0)),
            pl.BlockSpec((w, kvw), lambda b, i: (b * nb + i, kblk)),
            pl.BlockSpec((w, kvw), lambda b, i: (b * nb + i, kblk + 1)),
            pl.BlockSpec((w, LANES), lambda b, i: (i, 0)),
            pl.BlockSpec((w, LANES), lambda b, i: (i, 0)),
            pl.BlockSpec((1, LANES), lambda b, i: (0, 0)),
            pl.BlockSpec((1, LANES), lambda b, i: (0, 0)),
            pl.BlockSpec((LANES, LANES), lambda b, i: (0, 0)),
        ],
        out_specs=pl.BlockSpec((w, qw), lambda b, i: (b * nb + i, 0)),
        out_shape=jax.ShapeDtypeStruct((batch * seq, qw), BF16),
        scratch_shapes=[
            pltpu.VMEM((2, SWA_KV_HEADS, w, LANES), BF16),
            pltpu.VMEM((2, SWA_KV_HEADS, w, LANES), BF16),
        ],
        compiler_params=_cparams(("parallel", "arbitrary")),
        name="swa",
    )(sinks.astype(F32), proj, proj, proj, cos, sin,
      jnp.tile(q_norm.astype(F32), reps).reshape(1, LANES),
      jnp.tile(k_norm.astype(F32), reps).reshape(1, LANES), seg)


def _route(logits):
    tm = logits.shape[0]
    lane = lax.broadcasted_iota(jnp.int32, (tm, LANES), 1)
    lane_f = lane.astype(F32)
    big = float(LANES)
    neg = -jnp.inf
    gl = jnp.where(lane < N_GROUPS, logits, neg)
    gmax = jnp.max(gl, axis=1, keepdims=True)
    gidx = jnp.min(jnp.where(gl == gmax, lane_f, big), axis=1, keepdims=True)
    g_p = 1.0 / jnp.sum(jnp.exp(gl - gmax), axis=1, keepdims=True)
    egroup = lax.shift_right_arithmetic(lane - N_GROUPS, 3).astype(F32)
    in_group = (lane >= N_GROUPS) & (lane < N_GROUPS + N_EXPERTS) & (egroup == gidx)
    el = jnp.where(in_group, logits, neg)
    emax = jnp.max(el, axis=1, keepdims=True)
    ee = jnp.exp(el - emax)
    prob = ee / jnp.sum(ee, axis=1, keepdims=True)
    pm = jnp.where(in_group, prob, -1.0)
    p1 = jnp.max(pm, axis=1, keepdims=True)
    i1 = jnp.min(jnp.where(pm == p1, lane_f, big), axis=1, keepdims=True)
    pm2 = jnp.where(lane_f == i1, -1.0, pm)
    p2 = jnp.max(pm2, axis=1, keepdims=True)
    i2 = jnp.min(jnp.where(pm2 == p2, lane_f, big), axis=1, keepdims=True)
    den = p1 + p2
    out = jnp.where(lane == 0, i1 - N_GROUPS, 0.0)
    out = jnp.where(lane == 1, i2 - N_GROUPS, out)
    out = jnp.where(lane == 2, g_p * p1 / den, out)
    out = jnp.where(lane == 3, g_p * p2 / den, out)
    return out


def _out_proj_kernel(*refs, n_act):
    acts = refs[:n_act]
    ws = refs[n_act:2 * n_act]
    x_ref, g_ref, wr_ref, x1_ref, h_ref, route_ref, y_even, y_odd = refs[2 * n_act:]
    i = pl.program_id(0)
    tm = x_ref.shape[0]

    def step(prev, cur):
        y = prev[...]
        x1_ref[...] = y
        ms = jnp.mean(y * y, axis=-1, keepdims=True)
        h = y * lax.rsqrt(ms + EPS) * g_ref[...]
        for s in range(ROW_TILES):
            h_ref[pl.ds(s, tm, stride=ROW_TILES), :] = h[:, s * LANES:(s + 1) * LANES]
        h_hi, h_lo = _split_bf16(h)
        wr = wr_ref[...]
        lg2 = jnp.dot(h_hi, wr, preferred_element_type=F32) + jnp.dot(h_lo, wr, preferred_element_type=F32)
        route_ref[...] = _route(lg2[:, :LANES] + lg2[:, LANES:])
        acc = x_ref[...]
        for a, w in zip(acts, ws):
            acc = acc + jnp.dot(a[...], w[...], preferred_element_type=F32)
        cur[...] = acc

    @pl.when(i == 0)
    def _():
        y_odd[...] = jnp.zeros_like(y_odd)

    @pl.when(i % 2 == 0)
    def _():
        step(y_odd, y_even)

    @pl.when(i % 2 == 1)
    def _():
        step(y_even, y_odd)


def _out_proj(acts, ws, x, gain, w_router, *, tm=256):
    n, d = x.shape
    tm = min(tm, n)
    n_act = len(acts)
    nt = n // tm
    const = lambda i: (0, 0)
    cur = lambda i: (jnp.minimum(i, nt - 1), 0)
    prev = lambda i: (jnp.maximum(i - 1, 0), 0)
    in_specs = [pl.BlockSpec((tm, a.shape[1]), cur) for a in acts]
    in_specs += [pl.BlockSpec(w.shape, const, pipeline_mode=pl.Buffered(1)) for w in ws]
    in_specs += [
        pl.BlockSpec((tm, d), cur),
        pl.BlockSpec((1, d), const),
        pl.BlockSpec((d, 2 * LANES), const),
    ]
    return pl.pallas_call(
        functools.partial(_out_proj_kernel, n_act=n_act),
        grid=(nt + 1,),
        in_specs=in_specs,
        out_specs=[
            pl.BlockSpec((tm, d), prev),
            pl.BlockSpec((tm * ROW_TILES, LANES), prev),
            pl.BlockSpec((tm, LANES), prev),
        ],
        out_shape=[
            jax.ShapeDtypeStruct((n, d), F32),
            jax.ShapeDtypeStruct((n * ROW_TILES, LANES), F32),
            jax.ShapeDtypeStruct((n, LANES), F32),
        ],
        scratch_shapes=[pltpu.VMEM((tm, d), F32), pltpu.VMEM((tm, d), F32)],
        compiler_params=_cparams(("arbitrary",)),
        name="out_proj_router",
    )(*acts, *ws, x, gain.reshape(1, d), w_router)


def _row_gather_start(idx_ref, base, count, src_hbm, dst, sem, *, both_queues):
    group = 8

    def body(gi, carry):
        for j in range(group):
            r = gi * group + j
            tok = idx_ref[base + r]
            pltpu.make_async_copy(src_hbm.at[pl.ds(pl.multiple_of(tok * ROW_TILES, ROW_TILES), ROW_TILES)],
                                  dst.at[:, pl.ds(r, 1), :],
                                  sem).start(priority=j % 2 if both_queues else 0)
        return carry
    lax.fori_loop(0, count // group, body, 0)


def _row_gather_wait(dst, sem):
    pltpu.make_async_copy(dst, dst, sem).wait()


def _expert_kernel(be_ref, nu_ref, nxt_ref, tok_ref, h_hbm, wg_hbm, wu_hbm, wd_hbm, o_ref,
                   xbuf0, xbuf1, xbuf2, stage_g, stage_u, stage_d, wg_b, wu_b, wd_b, gsem, wsem, *, layer):
    i = pl.program_id(0)
    bm = MOE_BM
    nu = nu_ref[0]
    last_block = pl.num_programs(0) - 1
    e = be_ref[i]
    weight_queue = 1
    xbufs = (xbuf0, xbuf1, xbuf2)

    def weight_copies(expert):
        return (pltpu.make_async_copy(wg_hbm.at[layer, expert], stage_g, wsem.at[0]),
                pltpu.make_async_copy(wu_hbm.at[layer, expert], stage_u, wsem.at[1]),
                pltpu.make_async_copy(wd_hbm.at[layer, expert], stage_d, wsem.at[2]))

    @pl.when(i == 0)
    def _():
        _row_gather_start(tok_ref, 0, bm, h_hbm, xbuf0, gsem.at[0], both_queues=False)
        _row_gather_start(tok_ref, jnp.minimum(1, last_block) * bm, bm, h_hbm, xbuf1, gsem.at[1],
                          both_queues=False)
        for cp in weight_copies(e):
            cp.start(priority=weight_queue)

    def run_block(k):
        cur, cur_sem = xbufs[k], gsem.at[k]
        mid, mid_sem = xbufs[(k + 1) % 3], gsem.at[(k + 1) % 3]
        nxt_buf, nxt_sem = xbufs[(k + 2) % 3], gsem.at[(k + 2) % 3]
        first_of_run = jnp.logical_or(i == 0, e != be_ref[jnp.maximum(i - 1, 0)])

        @pl.when(first_of_run)
        def _():
            for cp, stage, dst in zip(weight_copies(e), (stage_g, stage_u, stage_d), (wg_b, wu_b, wd_b)):
                cp.wait()
                dst[...] = stage[...].astype(BF16)
            nxt = nxt_ref[e]

            @pl.when(nxt >= 0)
            def _():
                for cp in weight_copies(nxt):
                    cp.start(priority=weight_queue)

        _row_gather_wait(cur, cur_sem)
        base = jnp.minimum(i + 2, last_block) * bm
        for r in range(bm):
            tok = tok_ref[base + r]
            pltpu.make_async_copy(h_hbm.at[pl.ds(pl.multiple_of(tok * ROW_TILES, ROW_TILES), ROW_TILES)],
                                  nxt_buf.at[:, pl.ds(r, 1), :], nxt_sem).start()
        x = jnp.concatenate([cur[s].astype(BF16) for s in range(ROW_TILES)], axis=1)
        g = jnp.dot(x, wg_b[...], preferred_element_type=F32)
        u = jnp.dot(x, wu_b[...], preferred_element_type=F32)
        hid = (g * _sigmoid(g) * u).astype(BF16)
        y = jnp.dot(hid, wd_b[...], preferred_element_type=F32)
        for s in range(ROW_TILES):
            o_ref[pl.ds(s, bm, stride=ROW_TILES), :] = y[:, s * LANES:(s + 1) * LANES]

        @pl.when(i + 1 >= nu)
        def _():
            _row_gather_wait(mid, mid_sem)
            _row_gather_wait(nxt_buf, nxt_sem)

    for k in range(3):
        pl.when(jnp.logical_and(i < nu, i % 3 == k))(functools.partial(run_block, k))

    @pl.when(i >= nu)
    def _():
        o_ref[...] = jnp.zeros_like(o_ref)


def _expert_mlp(h_tiled, tok_pad, block_expert, n_used, next_expert, w_gate, w_up, w_down, layer):
    n_pad = tok_pad.shape[0]
    bm = MOE_BM
    _, _, d, ff = w_gate.shape
    any_spec = pl.BlockSpec(memory_space=pl.ANY)
    grid_spec = pltpu.PrefetchScalarGridSpec(
        num_scalar_prefetch=4,
        grid=(n_pad // bm,),
        in_specs=[any_spec, any_spec, any_spec, any_spec],
        out_specs=pl.BlockSpec((bm * ROW_TILES, LANES), lambda i, *_: (i, 0)),
        scratch_shapes=[
            pltpu.VMEM((ROW_TILES, bm, LANES), F32), pltpu.VMEM((ROW_TILES, bm, LANES), F32),
            pltpu.VMEM((ROW_TILES, bm, LANES), F32),
            pltpu.VMEM((d, ff), F32), pltpu.VMEM((d, ff), F32), pltpu.VMEM((ff, d), F32),
            pltpu.VMEM((d, ff), BF16), pltpu.VMEM((d, ff), BF16), pltpu.VMEM((ff, d), BF16),
            pltpu.SemaphoreType.DMA((3,)),
            pltpu.SemaphoreType.DMA((3,)),
        ],
    )
    return pl.pallas_call(
        functools.partial(_expert_kernel, layer=layer),
        grid_spec=grid_spec,
        out_shape=jax.ShapeDtypeStruct((n_pad * ROW_TILES, LANES), F32),
        compiler_params=_cparams(("arbitrary",)),
        name="expert_mlp",
    )(block_expert, n_used, next_expert, tok_pad, h_tiled.reshape(-1, 1, LANES), w_gate, w_up, w_down)


def _combine_kernel(dest_ref, x1_ref, route_ref, y_hbm, o_ref, ybuf0, ybuf1, ybuf2, sem):
    i = pl.program_id(0)
    nsteps = pl.num_programs(0)
    last = nsteps - 1
    tm = x1_ref.shape[0]
    ybufs = (ybuf0, ybuf1, ybuf2)

    def choice_base(k, step):
        return k * (nsteps * tm) + step * tm

    @pl.when(i == 0)
    def _():
        for step, buf in ((0, 0), (jnp.minimum(1, last), 1)):
            for k in range(2):
                _row_gather_start(dest_ref, choice_base(k, step), tm, y_hbm, ybufs[buf].at[k], sem.at[buf],
                                  both_queues=True)

    def run_step(b):
        cur = ybufs[b]
        nxt, nxt_sem = ybufs[(b + 2) % 3], sem.at[(b + 2) % 3]
        _row_gather_wait(cur, sem.at[b])
        ahead = jnp.minimum(i + 2, last)
        for k in range(2):
            base = choice_base(k, ahead)
            for r in range(tm):
                row = dest_ref[base + r]
                pltpu.make_async_copy(y_hbm.at[pl.ds(pl.multiple_of(row * ROW_TILES, ROW_TILES), ROW_TILES)],
                                      nxt.at[k, :, pl.ds(r, 1), :], nxt_sem).start(priority=r % 2)
        route = route_ref[...]
        g0 = jnp.broadcast_to(route[:, 2:3], (tm, LANES))
        g1 = jnp.broadcast_to(route[:, 3:4], (tm, LANES))
        for s in range(ROW_TILES):
            o_ref[:, s * LANES:(s + 1) * LANES] = (x1_ref[:, s * LANES:(s + 1) * LANES]
                                                   + g0 * cur[0, s] + g1 * cur[1, s])

        @pl.when(i == last)
        def _():
            _row_gather_wait(ybufs[(b + 1) % 3], sem.at[(b + 1) % 3])
            _row_gather_wait(nxt, nxt_sem)

    for b in range(3):
        pl.when(i % 3 == b)(functools.partial(run_step, b))


def _combine(x1, route, dest, y_tiled, *, tm=256):
    n, d = x1.shape
    tm = min(tm, n)
    grid_spec = pltpu.PrefetchScalarGridSpec(
        num_scalar_prefetch=1,
        grid=(n // tm,),
        in_specs=[
            pl.BlockSpec((tm, d), lambda i, *_: (i, 0)),
            pl.BlockSpec((tm, LANES), lambda i, *_: (i, 0)),
            pl.BlockSpec(memory_space=pl.ANY),
        ],
        out_specs=pl.BlockSpec((tm, d), lambda i, *_: (i, 0)),
        scratch_shapes=[
            pltpu.VMEM((2, ROW_TILES, tm, LANES), F32), pltpu.VMEM((2, ROW_TILES, tm, LANES), F32),
            pltpu.VMEM((2, ROW_TILES, tm, LANES), F32),
            pltpu.SemaphoreType.DMA((3,)),
        ],
    )
    return pl.pallas_call(
        _combine_kernel,
        grid_spec=grid_spec,
        out_shape=jax.ShapeDtypeStruct((n, d), F32),
        compiler_params=_cparams(("arbitrary",)),
        name="moe_combine",
    )(dest, x1, route, y_tiled.reshape(-1, 1, LANES))


def _moe(x1, h_tiled, route, w_gate, w_up, w_down, layer):
    n, d = x1.shape
    bm = MOE_BM
    n_assign = 2 * n
    expert_id = route[:, 0:2].astype(jnp.int32).reshape(-1)
    onehot = (expert_id[:, None] == jnp.arange(N_EXPERTS, dtype=jnp.int32)[None, :]).astype(jnp.int32)
    csum = jnp.cumsum(onehot, axis=0)
    rank = jnp.sum(csum * onehot, axis=1) - 1
    counts = csum[-1]
    padded = (counts + bm - 1) // bm * bm
    pend = jnp.cumsum(padded)
    pstart = pend - padded
    dest = (jnp.sum(onehot * pstart[None, :], axis=1) + rank).astype(jnp.int32)
    n_pad = (n_assign + N_EXPERTS * (bm - 1) + bm - 1) // bm * bm
    n_blk = n_pad // bm
    token_id = jnp.arange(n_assign, dtype=jnp.int32) // 2
    tok_pad = (jnp.arange(n_pad, dtype=jnp.int32) % n).at[dest].set(token_id)
    block_start = jnp.arange(n_blk, dtype=jnp.int32) * bm
    block_expert = jnp.minimum(jnp.sum(pend[None, :] <= block_start[:, None], axis=1), N_EXPERTS - 1)
    n_used = (pend[-1] // bm).astype(jnp.int32).reshape(1)
    ids = jnp.arange(N_EXPERTS, dtype=jnp.int32)
    cand = jnp.where((counts[None, :] > 0) & (ids[None, :] > ids[:, None]), ids[None, :], N_EXPERTS)
    next_expert = jnp.min(cand, axis=1)
    next_expert = jnp.where(next_expert >= N_EXPERTS, -1, next_expert).astype(jnp.int32)
    y_tiled = _expert_mlp(h_tiled, tok_pad, block_expert.astype(jnp.int32), n_used, next_expert,
                          w_gate, w_up, w_down, layer)
    dest_by_choice = dest.reshape(n, 2).T.reshape(-1).astype(jnp.int32)
    return _combine(x1, route, dest_by_choice, y_tiled)


def _router_weights(w_group, w_expert):
    d = w_group.shape[0]
    pad = jnp.zeros((d, LANES - N_GROUPS - N_EXPERTS), F32)
    w = jnp.concatenate([w_group.astype(F32), w_expert.astype(F32), pad], axis=1)
    w_hi = w.astype(BF16)
    w_lo = (w - w_hi.astype(F32)).astype(BF16)
    return jnp.concatenate([w_hi, w_lo], axis=1)


def kernel(x, norm_mix, norm_ffn, even_w_in, ret_norm, gdn_conv, gdn_a_log, gdn_dt_bias, gdn_norm,
           even_w_out, odd_w_in, q_norm, k_norm, attn_sinks, odd_w_out, router_group, router_expert,
           expert_w_gate, expert_w_up, expert_w_down):
    batch, seq, d = x.shape
    n = batch * seq
    xt = x.reshape(n, d)

    w_in = even_w_in[0]
    w_main = w_in[:, :EVEN_MAIN].astype(BF16)
    w_aux = jnp.pad(w_in[:, EVEN_MAIN:], ((0, 0), (0, LANES - 2 * GDN_HEADS))).astype(BF16)
    proj, aux = _norm_proj(xt, norm_mix[0], w_main, w_aux, tn=EVEN_MAIN // 8)
    cos_r, sin_r = _rope_tables(seq, RET_DK // 2)
    log_gamma = jnp.log1p(-jnp.exp2(-5.0 - jnp.arange(RET_HEADS, dtype=F32)))
    log_gamma = jnp.broadcast_to(log_gamma[:, None, None], (RET_HEADS, 1, LANES))
    o_ret = _retention(proj, cos_r, sin_r, log_gamma, ret_norm[0].astype(F32), batch, seq)
    lane_pad = (GDN_HEADS, LANES - 2 * GDN_HEADS)
    alog_row = jnp.pad(gdn_a_log[0].astype(F32), lane_pad).reshape(1, LANES)
    dtb_row = jnp.pad(gdn_dt_bias[0].astype(F32), lane_pad).reshape(1, LANES)
    o_gdn = _gdn(proj, aux, gdn_conv[0].astype(F32), alog_row, dtb_row, gdn_norm[0].astype(F32), batch, seq)
    w_out = even_w_out[0].astype(BF16)
    split = RET_HEADS * RET_DV
    x1, h2, route = _out_proj([o_ret, o_gdn], [w_out[:split], w_out[split:]], xt, norm_ffn[0],
                              _router_weights(router_group[0], router_expert[0]))
    xt = _moe(x1, h2, route, expert_w_gate, expert_w_up, expert_w_down, 0)

    proj = _norm_proj(xt, norm_mix[1], odd_w_in[0].astype(BF16), tn=ODD_IN // 2)
    cos_s, sin_s = _rope_tables(seq, SWA_HEAD_DIM // 2)
    o_swa = _swa(proj, cos_s, sin_s, q_norm[0], k_norm[0], attn_sinks[0], batch, seq)
    x1, h2, route = _out_proj([o_swa], [odd_w_out[0].astype(BF16)], xt, norm_ffn[1],
                              _router_weights(router_group[1], router_expert[1]))
    xt = _moe(x1, h2, route, expert_w_gate, expert_w_up, expert_w_down, 1)
    return xt.reshape(batch, seq, d)
```
